```python
import jax, jax.numpy as jnp
from jax import lax
import numpy as np

D_MODEL = 1024
BATCH = 8
SEQ = 4096
DEPTH = 2

D_MIX = D_MODEL
D_PLE = 256
D_FF = 4 * D_MODEL
EPS = 1e-6

ML_HEADS = 4
ML_DIM = D_MIX // 4
ML_HD = ML_DIM // ML_HEADS
ML_CHUNK = 64
ML_IN = 4 * ML_DIM + 2 * ML_HEADS

LRU_DIM = D_MIX // 4
LRU_BLOCKS = 4
LRU_BD = LRU_DIM // LRU_BLOCKS
CONV_W = 4
LRU_C = 8.0
LRU_IN = 2 * LRU_DIM

RW_DIM = D_MIX - ML_DIM - LRU_DIM
RW_HD = 64
RW_HEADS = RW_DIM // RW_HD
RW_W_LORA = 32
RW_A_LORA = 32
RW_G_LORA = 96
RW_IN = 3 * RW_DIM + RW_W_LORA + RW_A_LORA + RW_G_LORA
RW_GN_EPS = 64e-5

D_IN_PROJ = ML_IN + LRU_IN + RW_IN

kernel_name = "hybrid_mlstm_rglru_rwkv7_block"


def rms_norm(x, gain):
    x32 = x.astype(jnp.float32)
    y = x32 * lax.rsqrt(jnp.mean(x32 * x32, axis=-1, keepdims=True) + EPS) * gain
    return y.astype(x.dtype)


def token_shift(z):
    return jnp.pad(z, ((0, 0), (1, 0), (0, 0)))[:, :-1]


def mlstm_chunkwise(q, k, v, i_pre, f_pre):
    B, S, H, Dh = q.shape
    L = ML_CHUNK
    NC = S // L
    to_chunks = lambda t: t.reshape(B, NC, L, H, -1).transpose(0, 3, 1, 2, 4)
    qc = to_chunks(q) * (Dh ** -0.5)
    kc = to_chunks(k)
    vc = to_chunks(v)
    ig = to_chunks(i_pre[..., None])[..., 0]
    lf = to_chunks(jax.nn.log_sigmoid(f_pre)[..., None])[..., 0]
    F = jnp.cumsum(lf, axis=-1)
    F_tot = F[..., -1]

    lw = F_tot[..., None] - F + ig
    m_loc = jnp.max(lw, axis=-1)
    wgt = jnp.exp(lw - m_loc[..., None])
    C_loc = jnp.einsum('bhcl,bhclv,bhclk->bhcvk', wgt, vc, kc)
    n_loc = jnp.einsum('bhcl,bhclk->bhck', wgt, kc)

    def carry_state(state, inp):
        C, n, m = state
        C_l, n_l, m_l, f_tot = inp
        m_new = jnp.maximum(f_tot + m, m_l)
        a = jnp.exp(f_tot + m - m_new)
        b = jnp.exp(m_l - m_new)
        new = (a[..., None, None] * C + b[..., None, None] * C_l,
               a[..., None] * n + b[..., None] * n_l,
               m_new)
        return new, (C, n, m)

    cm = lambda t: jnp.moveaxis(t, 2, 0)
    init = (jnp.zeros((B, H, Dh, Dh), jnp.float32),
            jnp.zeros((B, H, Dh), jnp.float32),
            jnp.zeros((B, H), jnp.float32))
    _, (C_in, n_in, m_in) = lax.scan(carry_state, init,
                                     (cm(C_loc), cm(n_loc), cm(m_loc), cm(F_tot)))
    C_in = jnp.moveaxis(C_in, 0, 2)
    n_in = jnp.moveaxis(n_in, 0, 2)
    m_in = jnp.moveaxis(m_in, 0, 2)

    causal = jnp.tril(jnp.ones((L, L), dtype=bool))
    log_d = jnp.where(causal, F[..., :, None] - F[..., None, :] + ig[..., None, :], -jnp.inf)
    m_inter = F + m_in[..., None]
    m_row = jnp.maximum(m_inter, jnp.max(log_d, axis=-1))
    scores = jnp.einsum('bhcjd,bhcsd->bhcjs', qc, kc) * jnp.exp(log_d - m_row[..., None])
    w_inter = jnp.exp(m_inter - m_row)
    num = (jnp.einsum('bhcjs,bhcsv->bhcjv', scores, vc)
           + w_inter[..., None] * jnp.einsum('bhcvk,bhcjk->bhcjv', C_in, qc))
    den = jnp.sum(scores, axis=-1) + w_inter * jnp.einsum('bhck,bhcjk->bhcj', n_in, qc)
    h = num / jnp.maximum(jnp.abs(den), jnp.exp(-m_row))[..., None]
    return h.transpose(0, 2, 3, 1, 4).reshape(B, S, H, Dh)


def mlstm_group(z, gate_bias, head_norm):
    B, S, _ = z.shape
    z = z.astype(jnp.float32)
    q, k, v, o, i_pre, f_pre = jnp.split(
        z, [ML_DIM, 2 * ML_DIM, 3 * ML_DIM, 4 * ML_DIM, 4 * ML_DIM + ML_HEADS], axis=-1)
    hv = lambda t: t.reshape(B, S, ML_HEADS, ML_HD)
    h = mlstm_chunkwise(hv(q), hv(k), hv(v), i_pre + gate_bias[0], f_pre + gate_bias[1])
    h = rms_norm(h, head_norm.reshape(ML_HEADS, ML_HD)).reshape(B, S, ML_DIM)
    return h * jax.nn.sigmoid(o)


def rglru_group(z, conv_w, conv_b, w_r, b_r, w_i, b_i, lam):
    B, S, _ = z.shape
    z = z.astype(jnp.float32)
    xb, gb = jnp.split(z, [LRU_DIM], axis=-1)
    xc = lax.conv_general_dilated(
        xb, conv_w[:, None, :].astype(xb.dtype), window_strides=(1,),
        padding=[(CONV_W - 1, 0)], dimension_numbers=('NWC', 'WIO', 'NWC'),
        feature_group_count=LRU_DIM) + conv_b
    xblk = xc.reshape(B, S, LRU_BLOCKS, LRU_BD)
    r = jax.nn.sigmoid(jnp.einsum('bsnd,nde->bsne', xblk, w_r).reshape(B, S, LRU_DIM) + b_r)
    i = jax.nn.sigmoid(jnp.einsum('bsnd,nde->bsne', xblk, w_i).reshape(B, S, LRU_DIM) + b_i)
    log_a = -LRU_C * r * jax.nn.softplus(-lam)
    a = jnp.exp(log_a)
    u = jnp.sqrt(-jnp.expm1(2.0 * log_a)) * (i * xc)

    def combine(left, right):
        a1, b1 = left
        a2, b2 = right
        return a1 * a2, a2 * b1 + b2

    _, h = lax.associative_scan(combine, (a, u), axis=1)
    return h * jax.nn.gelu(gb)


def rwkv7_recurrence(r, decay, k, v, kk, a):
    B, S, H, Dh = r.shape

    def step(state, inp):
        r_t, w_t, k_t, v_t, kk_t, a_t = inp
        sa = jnp.einsum('bhvk,bhk->bhv', state, -kk_t)
        state = (state * w_t[:, :, None, :]
                 + sa[..., None] * (kk_t * a_t)[:, :, None, :]
                 + v_t[..., None] * k_t[:, :, None, :])
        return state, jnp.einsum('bhvk,bhk->bhv', state, r_t)

    init = jnp.zeros((B, H, Dh, Dh), jnp.float32)
    xs = tuple(jnp.moveaxis(t, 1, 0) for t in (r, decay, k, v, kk, a))
    _, out = lax.scan(step, init, xs)
    return jnp.moveaxis(out, 0, 1)


def rwkv7_group(z, mu, w0, w2, a0, a2, g2, k_k, k_a, r_k, ln_w, ln_b):
    B, S, _ = z.shape
    z = z.astype(jnp.float32)
    z = z + (token_shift(z) - z) * mu
    r, k, v, wd, ad, gd = jnp.split(
        z, [RW_DIM, 2 * RW_DIM, 3 * RW_DIM, 3 * RW_DIM + RW_W_LORA,
            3 * RW_DIM + RW_W_LORA + RW_A_LORA], axis=-1)
    w = -jax.nn.softplus(-(w0 + jnp.tanh(wd) @ w2)) - 0.5
    decay = jnp.exp(-jnp.exp(w))
    a = jax.nn.sigmoid(a0 + ad @ a2)
    g = jax.nn.sigmoid(gd) @ g2
    hv = lambda t: t.reshape(B, S, RW_HEADS, RW_HD)
    kk = hv(k * k_k)
    kk = kk / jnp.maximum(jnp.linalg.norm(kk, axis=-1, keepdims=True), 1e-12)
    k = k * (1.0 + (a - 1.0) * k_a)
    rh, kh, vh = hv(r), hv(k), hv(v)
    o = rwkv7_recurrence(rh, hv(decay), kh, vh, kk, hv(a))
    mean = jnp.mean(o, axis=-1, keepdims=True)
    var = jnp.mean(jnp.square(o - mean), axis=-1, keepdims=True)
    o = ((o - mean) * lax.rsqrt(var + RW_GN_EPS)).reshape(B, S, RW_DIM) * ln_w + ln_b
    bonus = jnp.sum(rh * kh * r_k, axis=-1, keepdims=True) * vh
    return (o + bonus.reshape(B, S, RW_DIM)) * g


def setup_inputs(seed: int = 0) -> dict:
    key = jax.random.key(seed)
    ks = iter(jax.random.split(key, 48))
    nrm = lambda shape, scale: jax.random.normal(next(ks), shape, jnp.float32) * scale
    gain = lambda shape: 1.0 + nrm(shape, 0.05)
    u = jax.random.uniform(next(ks), (DEPTH, LRU_DIM), jnp.float32, 0.9, 0.999)
    s = u ** (1.0 / LRU_C)
    lam = jnp.log(s) - jnp.log1p(-s)
    i_bias = -1.0 + nrm((DEPTH, 1, ML_HEADS), 0.1)
    f_bias = jax.random.uniform(next(ks), (DEPTH, 1, ML_HEADS), jnp.float32, 3.0, 6.0)
    return {
        "x": nrm((BATCH, SEQ, D_MODEL), 1.0),
        "p": nrm((DEPTH, BATCH, SEQ, D_PLE), 1.0),
        "norm_mix_pre": gain((DEPTH, D_MODEL)),
        "norm_mix_post": gain((DEPTH, D_MODEL)),
        "norm_ffn_pre": gain((DEPTH, D_MODEL)),
        "norm_ffn_post": gain((DEPTH, D_MODEL)),
        "norm_ple": gain((DEPTH, D_MODEL)),
        "w_in": nrm((DEPTH, D_MODEL, D_IN_PROJ), D_MODEL ** -0.5),
        "w_out": nrm((DEPTH, D_MIX, D_MODEL), D_MIX ** -0.5),
        "ml_gate_bias": jnp.concatenate([i_bias, f_bias], axis=1),
        "ml_head_norm": gain((DEPTH, ML_DIM)),
        "lru_conv_w": nrm((DEPTH, CONV_W, LRU_DIM), CONV_W ** -0.5),
        "lru_conv_b": nrm((DEPTH, LRU_DIM), 0.02),
        "lru_w_r": nrm((DEPTH, LRU_BLOCKS, LRU_BD, LRU_BD), LRU_BD ** -0.5),
        "lru_b_r": nrm((DEPTH, LRU_DIM), 0.1),
        "lru_w_i": nrm((DEPTH, LRU_BLOCKS, LRU_BD, LRU_BD), LRU_BD ** -0.5),
        "lru_b_i": nrm((DEPTH, LRU_DIM), 0.1),
        "lru_lambda": lam,
        "rw_mu": jax.random.uniform(next(ks), (DEPTH, RW_IN), jnp.float32),
        "rw_w0": jax.random.uniform(next(ks), (DEPTH, RW_DIM), jnp.float32, -2.0, 1.0),
        "rw_w2": nrm((DEPTH, RW_W_LORA, RW_DIM), 0.1 * RW_W_LORA ** -0.5),
        "rw_a0": nrm((DEPTH, RW_DIM), 0.1),
        "rw_a2": nrm((DEPTH, RW_A_LORA, RW_DIM), 0.1 * RW_A_LORA ** -0.5),
        "rw_g2": nrm((DEPTH, RW_G_LORA, RW_DIM), RW_G_LORA ** -0.5),
        "rw_k_k": 0.85 + nrm((DEPTH, RW_DIM), 0.05),
        "rw_k_a": 1.0 + nrm((DEPTH, RW_DIM), 0.05),
        "rw_r_k": nrm((DEPTH, RW_HEADS, RW_HD), 0.1),
        "rw_ln_w": gain((DEPTH, RW_DIM)),
        "rw_ln_b": nrm((DEPTH, RW_DIM), 0.02),
        "ffn_w1": nrm((DEPTH, D_MODEL, D_FF), D_MODEL ** -0.5),
        "ffn_w2": nrm((DEPTH, D_FF, D_MODEL), D_FF ** -0.5),
        "ple_w_proj": nrm((DEPTH, D_PLE, D_MODEL), D_PLE ** -0.5),
        "ple_w_gate": nrm((DEPTH, D_MODEL, D_MODEL), D_MODEL ** -0.5),
    }


def reference(x, p, norm_mix_pre, norm_mix_post, norm_ffn_pre, norm_ffn_post, norm_ple,
              w_in, w_out, ml_gate_bias, ml_head_norm,
              lru_conv_w, lru_conv_b, lru_w_r, lru_b_r, lru_w_i, lru_b_i, lru_lambda,
              rw_mu, rw_w0, rw_w2, rw_a0, rw_a2, rw_g2, rw_k_k, rw_k_a, rw_r_k, rw_ln_w, rw_ln_b,
              ffn_w1, ffn_w2, ple_w_proj, ple_w_gate):
    for l in range(DEPTH):
        h = rms_norm(x, norm_mix_pre[l])
        z = h @ w_in[l]
        z_ml = z[..., :ML_IN]
        z_lru = z[..., ML_IN:ML_IN + LRU_IN]
        z_rw = z[..., ML_IN + LRU_IN:]
        y_ml = mlstm_group(z_ml, ml_gate_bias[l], ml_head_norm[l])
        y_lru = rglru_group(z_lru, lru_conv_w[l], lru_conv_b[l], lru_w_r[l], lru_b_r[l],
                            lru_w_i[l], lru_b_i[l], lru_lambda[l])
        y_rw = rwkv7_group(z_rw, rw_mu[l], rw_w0[l], rw_w2[l], rw_a0[l], rw_a2[l], rw_g2[l],
                           rw_k_k[l], rw_k_a[l], rw_r_k[l], rw_ln_w[l], rw_ln_b[l])
        mix = jnp.concatenate([y_ml, y_lru, y_rw], axis=-1).astype(x.dtype) @ w_out[l]
        x = x + rms_norm(mix, norm_mix_post[l])
        h = rms_norm(x, norm_ffn_pre[l])
        f = jnp.square(jax.nn.relu(h @ ffn_w1[l])) @ ffn_w2[l]
        x = x + rms_norm(f, norm_ffn_post[l])
        e = jax.nn.sigmoid(x @ ple_w_gate[l]) * (p[l] @ ple_w_proj[l])
        x = x + rms_norm(e, norm_ple[l])
    return x
```

```python
import functools

import jax
import jax.numpy as jnp
from jax import lax
from jax.experimental import pallas as pl
from jax.experimental.pallas import tpu as pltpu

F32 = jnp.float32
BF16 = jnp.bfloat16

EPS = 1e-6
HD = 64
ML_HEADS = 4
ML_DIM = ML_HEADS * HD
LRU_DIM = 256
LRU_C = 8.0
CONV_W = 4
RW_HEADS = 8
RW_DIM = RW_HEADS * HD
RW_LORA = 160
RW_LORA_PAD = 256
RW_IN_PAD = 3 * RW_DIM + RW_LORA_PAD
RW_GN_EPS = 64e-5
CHUNK = 64
GATE_PAD = 128

VMEM_LIMIT = 48 * 1024 * 1024

_NT = (((1,), (1,)), ((), ()))
_TN = (((0,), (0,)), ((), ()))


def _mm(a, b):
    return jnp.dot(a.astype(BF16), b.astype(BF16), preferred_element_type=F32)


def _mm_nt(a, b):
    return lax.dot_general(a.astype(BF16), b.astype(BF16), _NT, preferred_element_type=F32)


def _mm_tn(a, b):
    return lax.dot_general(a.astype(BF16), b.astype(BF16), _TN, preferred_element_type=F32)


def _mm_f32(a, b):
    return jnp.dot(a, b, preferred_element_type=F32, precision=lax.Precision.HIGHEST)


def _rms(x, gain):
    return x * lax.rsqrt(jnp.mean(x * x, axis=-1, keepdims=True) + EPS) * gain


def _softplus(x):
    return jnp.maximum(x, 0.0) + jnp.log(1.0 + jnp.exp(-jnp.abs(x)))


def _log_sigmoid(x):
    return jnp.minimum(x, 0.0) - jnp.log(1.0 + jnp.exp(-jnp.abs(x)))


def _iota2(shape, dim):
    return lax.broadcasted_iota(jnp.int32, shape, dim)


def _params(*sem):
    return pltpu.CompilerParams(dimension_semantics=sem, vmem_limit_bytes=VMEM_LIMIT)


def _in_proj_kernel(x_ref, g_ref, wml_ref, wlru_ref, wrw_ref, wg_ref, wgt_ref,
                    zml_ref, zlru_ref, zrw_ref, gcol_ref, grow_ref):
    hb = _rms(x_ref[...], g_ref[...]).astype(BF16)
    zml_ref[...] = jnp.dot(hb, wml_ref[...], preferred_element_type=F32)
    zlru_ref[...] = jnp.dot(hb, wlru_ref[...], preferred_element_type=F32)
    zrw_ref[...] = jnp.dot(hb, wrw_ref[...], preferred_element_type=F32)
    gcol_ref[...] = jnp.dot(hb, wg_ref[...], preferred_element_type=F32)
    grow_ref[...] = lax.dot_general(wgt_ref[...], hb, _NT, preferred_element_type=F32)


def _in_proj(x2, gain, wml, wlru, wrw, wg, wgt, tm=256):
    t, d = x2.shape
    const = lambda i: (0, 0)
    row = lambda i: (i, 0)
    outs = (
        jax.ShapeDtypeStruct((t, wml.shape[1]), F32),
        jax.ShapeDtypeStruct((t, wlru.shape[1]), F32),
        jax.ShapeDtypeStruct((t, wrw.shape[1]), F32),
        jax.ShapeDtypeStruct((t, GATE_PAD), F32),
        jax.ShapeDtypeStruct((wgt.shape[0], t), F32),
    )
    return pl.pallas_call(
        _in_proj_kernel,
        grid=(t // tm,),
        in_specs=[
            pl.BlockSpec((tm, d), row),
            pl.BlockSpec((1, d), const),
            pl.BlockSpec(wml.shape, const),
            pl.BlockSpec(wlru.shape, const),
            pl.BlockSpec(wrw.shape, const),
            pl.BlockSpec(wg.shape, const),
            pl.BlockSpec(wgt.shape, const),
        ],
        out_specs=(
            pl.BlockSpec((tm, wml.shape[1]), row),
            pl.BlockSpec((tm, wlru.shape[1]), row),
            pl.BlockSpec((tm, wrw.shape[1]), row),
            pl.BlockSpec((tm, GATE_PAD), row),
            pl.BlockSpec((wgt.shape[0], tm), lambda i: (0, i)),
        ),
        out_shape=outs,
        compiler_params=_params("arbitrary"),
    )(x2, gain, wml, wlru, wrw, wg, wgt)


def _mlstm_kernel(z_ref, gcol_ref, grow_ref, bcol_ref, brow_ref, hn_ref, y_ref,
                  c_sc, n_sc, m_sc):
    L = CHUNK

    @pl.when(pl.program_id(1) == 0)
    def _():
        c_sc[...] = jnp.zeros_like(c_sc)
        n_sc[...] = jnp.zeros_like(n_sc)
        m_sc[...] = jnp.zeros_like(m_sc)

    ri = _iota2((L, L), 0)
    ci = _iota2((L, L), 1)
    causal = ri >= ci
    tril = causal.astype(F32)
    triu = (ri <= ci).astype(F32)

    gc = gcol_ref[...] + bcol_ref[...]
    f_col = _mm_f32(tril, _log_sigmoid(gc))
    gr = grow_ref[...] + brow_ref[...]
    f_row = _mm_f32(_log_sigmoid(gr), triu)

    scale = HD ** -0.5
    outs = []
    for h in range(ML_HEADS):
        q = z_ref[:, h * HD:(h + 1) * HD] * scale
        k = z_ref[:, ML_DIM + h * HD:ML_DIM + (h + 1) * HD]
        v = z_ref[:, 2 * ML_DIM + h * HD:2 * ML_DIM + (h + 1) * HD]
        o = z_ref[:, 3 * ML_DIM + h * HD:3 * ML_DIM + (h + 1) * HD]
        fc = f_col[:, ML_HEADS + h:ML_HEADS + h + 1]
        ic = gc[:, h:h + 1]
        g_row = f_row[ML_HEADS + h:ML_HEADS + h + 1, :] - gr[h:h + 1, :]
        c_in = c_sc[h]
        n_in = n_sc[h][0:1, :]
        m_in = m_sc[h][0:1, 0:1]

        log_d = jnp.where(causal, fc - g_row, -jnp.inf)
        m_inter = fc + m_in
        m_row = jnp.maximum(m_inter, jnp.max(log_d, axis=-1, keepdims=True))
        scores = _mm_nt(q, k) * jnp.exp(log_d - m_row)
        w_inter = jnp.exp(m_inter - m_row)
        num = _mm(scores, v) + w_inter * _mm(q, c_in)
        den = (jnp.sum(scores, axis=-1, keepdims=True)
               + w_inter * jnp.sum(q * n_in, axis=-1, keepdims=True))
        hh = num / jnp.maximum(jnp.abs(den), jnp.exp(-m_row))
        hh = _rms(hh, hn_ref[:, h * HD:(h + 1) * HD])
        outs.append(hh * jax.nn.sigmoid(o))

        f_tot = fc[L - 1:L, :]
        lw = f_tot - fc + ic
        m_loc = jnp.max(lw, axis=0, keepdims=True)
        kw = k * jnp.exp(lw - m_loc)
        m_new = jnp.maximum(f_tot + m_in, m_loc)
        a = jnp.exp(f_tot + m_in - m_new)
        b = jnp.exp(m_loc - m_new)
        c_sc[h] = a * c_in + b * _mm_tn(kw, v)
        n_new = a * n_in + b * jnp.sum(kw, axis=0, keepdims=True)
        n_sc[h] = jnp.broadcast_to(n_new, n_sc.shape[1:])
        m_sc[h] = jnp.broadcast_to(m_new, m_sc.shape[1:])

    y_ref[...] = jnp.concatenate(outs, axis=-1)


def _mlstm(zml, gcol, grow, bcol, brow, head_norm):
    b, s, _ = zml.shape
    L = CHUNK
    const = lambda i, j: (0, 0)
    return pl.pallas_call(
        _mlstm_kernel,
        grid=(b, s // L),
        in_specs=[
            pl.BlockSpec((None, L, 4 * ML_DIM), lambda i, j: (i, j, 0)),
            pl.BlockSpec((None, L, GATE_PAD), lambda i, j: (i, j, 0)),
            pl.BlockSpec((None, None, 2 * ML_HEADS, L), lambda i, j: (i, j, 0, 0)),
            pl.BlockSpec((1, GATE_PAD), const),
            pl.BlockSpec((2 * ML_HEADS, 1), const),
            pl.BlockSpec((1, ML_DIM), const),
        ],
        out_specs=pl.BlockSpec((None, L, ML_DIM), lambda i, j: (i, j, 0)),
        out_shape=jax.ShapeDtypeStruct((b, s, ML_DIM), F32),
        scratch_shapes=[
            pltpu.VMEM((ML_HEADS, HD, HD), F32),
            pltpu.VMEM((ML_HEADS, 8, HD), F32),
            pltpu.VMEM((ML_HEADS, 8, 128), F32),
        ],
        compiler_params=_params("arbitrary", "arbitrary"),
    )(zml, gcol, grow, bcol, brow, head_norm)


def _shift_rows(x, d, fill, row):
    return jnp.where(row >= d, pltpu.roll(x, d, 0), fill)


def _lru_kernel(z_ref, cw_ref, cb_ref, wg_ref, bg_ref, lam_ref, y_ref, xbuf, h_sc):
    ts = z_ref.shape[0]

    @pl.when(pl.program_id(1) == 0)
    def _():
        xbuf[0:8, :] = jnp.zeros((8, LRU_DIM), F32)
        h_sc[...] = jnp.zeros_like(h_sc)

    xb = z_ref[:, 0:LRU_DIM]
    gb = z_ref[:, LRU_DIM:2 * LRU_DIM]
    xbuf[8:8 + ts, :] = xb
    xc = cb_ref[...] + cw_ref[CONV_W - 1:CONV_W, :] * xb
    for j in range(1, CONV_W):
        xc = xc + cw_ref[CONV_W - 1 - j:CONV_W - j, :] * xbuf[8 - j:8 - j + ts, :]
    xbuf[0:8, :] = xb[ts - 8:ts, :]

    gates = _mm(xc, wg_ref[...]) + bg_ref[...]
    r = jax.nn.sigmoid(gates[:, 0:LRU_DIM])
    i = jax.nn.sigmoid(gates[:, LRU_DIM:2 * LRU_DIM])
    log_a = -LRU_C * r * _softplus(-lam_ref[...])
    a = jnp.exp(log_a)
    u = jnp.sqrt(-jnp.tanh(log_a) * (a * a + 1.0)) * (i * xc)

    row = _iota2((ts, LRU_DIM), 0)
    acc_a, acc_h = a, u
    d = 1
    while d < ts:
        acc_h = acc_a * _shift_rows(acc_h, d, 0.0, row) + acc_h
        acc_a = acc_a * _shift_rows(acc_a, d, 1.0, row)
        d *= 2
    hfull = acc_a * h_sc[0:1, :] + acc_h
    h_sc[...] = jnp.broadcast_to(hfull[ts - 1:ts, :], h_sc.shape)

    gelu = 0.5 * gb * (1.0 + jnp.tanh(0.7978845608028654 * (gb + 0.044715 * (gb * gb * gb))))
    y_ref[...] = hfull * gelu


def _rglru(zlru, conv_w, conv_b, wg, bg, lam, ts=512):
    b, s, _ = zlru.shape
    ts = min(ts, s)
    const = lambda i, j: (0, 0)
    return pl.pallas_call(
        _lru_kernel,
        grid=(b, s // ts),
        in_specs=[
            pl.BlockSpec((None, ts, 2 * LRU_DIM), lambda i, j: (i, j, 0)),
            pl.BlockSpec((CONV_W, LRU_DIM), const),
            pl.BlockSpec((1, LRU_DIM), const),
            pl.BlockSpec((LRU_DIM, 2 * LRU_DIM), const),
            pl.BlockSpec((1, 2 * LRU_DIM), const),
            pl.BlockSpec((1, LRU_DIM), const),
        ],
        out_specs=pl.BlockSpec((None, ts, LRU_DIM), lambda i, j: (i, j, 0)),
        out_shape=jax.ShapeDtypeStruct((b, s, LRU_DIM), F32),
        scratch_shapes=[
            pltpu.VMEM((ts + 8, LRU_DIM), F32),
            pltpu.VMEM((8, LRU_DIM), F32),
        ],
        compiler_params=_params("arbitrary", "arbitrary"),
    )(zlru, conv_w, conv_b, wg, bg, lam)


def _unit_lower_inverse(a_strict, ri, ci):
    L = a_strict.shape[0]
    eye = (ri == ci).astype(F32)
    bs = 1
    t = None
    while bs < L:
        off = ((ri // (2 * bs)) == (ci // (2 * bs))) & ((ri // bs) % 2 == 1) & ((ci // bs) % 2 == 0)
        a_off = jnp.where(off, a_strict, 0.0)
        if t is None:
            t = eye - a_off
        else:
            t = t - _mm(t, _mm(a_off, t))
        bs *= 2
    return t


def _rwkv_kernel(z_ref, mu_ref, wl_ref, w0_ref, a0_ref, kk_ref, ka_ref, rk_ref, lnw_ref, lnb_ref,
                 hsum_ref, y_ref, prev_sc, p_sc):
    L = CHUNK

    @pl.when(pl.program_id(1) == 0)
    def _():
        prev_sc[...] = jnp.zeros_like(prev_sc)
        p_sc[...] = jnp.zeros_like(p_sc)

    z = z_ref[...]
    row = _iota2(z.shape, 0)
    zprev = jnp.where(row == 0, prev_sc[0:1, :], pltpu.roll(z, 1, 0))
    prev_sc[...] = jnp.broadcast_to(z[L - 1:L, :], prev_sc.shape)
    z = z + (zprev - z) * mu_ref[...]

    r = z[:, 0:RW_DIM]
    k = z[:, RW_DIM:2 * RW_DIM]
    v = z[:, 2 * RW_DIM:3 * RW_DIM]
    lo = z[:, 3 * RW_DIM:3 * RW_DIM + RW_LORA_PAD]
    lane = _iota2(lo.shape, 1)
    lo = jnp.where(lane < 32, jnp.tanh(lo), jnp.where(lane < 64, lo, jax.nn.sigmoid(lo)))
    lora = _mm(lo, wl_ref[...])
    w = -_softplus(-(w0_ref[...] + lora[:, 0:RW_DIM])) - 0.5
    log_w = -jnp.exp(w)
    a = jax.nn.sigmoid(a0_ref[...] + lora[:, RW_DIM:2 * RW_DIM])
    g = lora[:, 2 * RW_DIM:3 * RW_DIM]

    hsum = hsum_ref[...]
    kk = k * kk_ref[...]
    kk = kk / jnp.maximum(jnp.sqrt(_mm(kk * kk, hsum)), 1e-12)
    k = k * (1.0 + (a - 1.0) * ka_ref[...])
    bonus = _mm(r * k * rk_ref[...], hsum) * v
    b = kk * a

    ri = _iota2((L, L), 0)
    ci = _iota2((L, L), 1)
    lower = ri >= ci
    strict = ri > ci
    eye = (ri == ci).astype(F32)

    cum = _mm_f32(lower.astype(F32), log_w)
    cum_end = cum[L - 1:L, :]
    g_fwd = jnp.exp(cum)
    g_inv = jnp.exp(-cum)
    g_end = jnp.exp(cum_end - cum)
    r_t = r * g_fwd
    k_t = k * g_inv
    b_t = b * g_inv
    kp_t = kk * jnp.exp(cum - log_w)
    k_e = k * g_end
    b_e = b * g_end
    decay_end = jnp.exp(cum_end)

    outs = []
    for h in range(RW_HEADS):
        sl = slice(h * HD, (h + 1) * HD)
        rh, kh, bh, kph, vh = r_t[:, sl], k_t[:, sl], b_t[:, sl], kp_t[:, sl], v[:, sl]
        a_b = jnp.where(strict, _mm_nt(kph, bh), 0.0)
        a_k = jnp.where(strict, _mm_nt(kph, kh), 0.0)
        q_k = jnp.where(lower, _mm_nt(rh, kh), 0.0)
        q_b = jnp.where(lower, _mm_nt(rh, bh), 0.0)
        t_inv = _unit_lower_inverse(a_b, ri, ci)
        w1 = _mm(t_inv, kph)
        w2 = _mm(t_inv, _mm(a_k, vh))
        r_hat = rh - _mm(q_b, w1)
        o_loc = _mm(q_k, vh) - _mm(q_b, w2)
        m_st = eye * decay_end[:, sl] - _mm_tn(b_e[:, sl], w1)
        n_st = _mm_tn(k_e[:, sl], vh) - _mm_tn(b_e[:, sl], w2)
        p = p_sc[h]
        outs.append(_mm(r_hat, p) + o_loc)
        p_sc[h] = _mm(m_st, p) + n_st
    o = jnp.concatenate(outs, axis=-1)

    mean = _mm(o, hsum) * (1.0 / HD)
    dev = o - mean
    var = _mm(dev * dev, hsum) * (1.0 / HD)
    o = dev * lax.rsqrt(var + RW_GN_EPS) * lnw_ref[...] + lnb_ref[...]
    y_ref[...] = (o + bonus) * g


def _rwkv(zrw, mu, wl, w0, a0, k_k, k_a, r_k, ln_w, ln_b, hsum):
    b, s, _ = zrw.shape
    L = CHUNK
    const = lambda i, j: (0, 0)
    vec = pl.BlockSpec((1, RW_DIM), const)
    return pl.pallas_call(
        _rwkv_kernel,
        grid=(b, s // L),
        in_specs=[
            pl.BlockSpec((None, L, RW_IN_PAD), lambda i, j: (i, j, 0)),
            pl.BlockSpec((1, RW_IN_PAD), const),
            pl.BlockSpec((RW_LORA_PAD, 3 * RW_DIM), const),
            vec, vec, vec, vec, vec, vec, vec,
            pl.BlockSpec((RW_DIM, RW_DIM), const),
        ],
        out_specs=pl.BlockSpec((None, L, RW_DIM), lambda i, j: (i, j, 0)),
        out_shape=jax.ShapeDtypeStruct((b, s, RW_DIM), F32),
        scratch_shapes=[
            pltpu.VMEM((8, RW_IN_PAD), F32),
            pltpu.VMEM((RW_HEADS, HD, HD), F32),
        ],
        compiler_params=_params("arbitrary", "arbitrary"),
    )(zrw, mu, wl, w0, a0, k_k, k_a, r_k, ln_w, ln_b, hsum)


def _out_proj_kernel(yml_ref, ylru_ref, yrw_ref, x_ref, w_ref, g_ref, o_ref):
    mix = (_mm(yml_ref[...], w_ref[0:ML_DIM, :])
           + _mm(ylru_ref[...], w_ref[ML_DIM:ML_DIM + LRU_DIM, :])
           + _mm(yrw_ref[...], w_ref[ML_DIM + LRU_DIM:, :]))
    o_ref[...] = x_ref[...] + _rms(mix, g_ref[...])


def _out_proj(yml, ylru, yrw, x2, w, gain, tm=512):
    t, d = x2.shape
    const = lambda i: (0, 0)
    row = lambda i: (i, 0)
    return pl.pallas_call(
        _out_proj_kernel,
        grid=(t // tm,),
        in_specs=[
            pl.BlockSpec((tm, ML_DIM), row),
            pl.BlockSpec((tm, LRU_DIM), row),
            pl.BlockSpec((tm, RW_DIM), row),
            pl.BlockSpec((tm, d), row),
            pl.BlockSpec(w.shape, const),
            pl.BlockSpec((1, d), const),
        ],
        out_specs=pl.BlockSpec((tm, d), row),
        out_shape=jax.ShapeDtypeStruct((t, d), F32),
        compiler_params=_params("arbitrary"),
    )(yml, ylru, yrw, x2, w, gain)


def _ffn_kernel(x_ref, gpre_ref, w1_ref, w2_ref, gpost_ref, o_ref, h_sc, acc_sc):
    j = pl.program_id(1)

    @pl.when(j == 0)
    def _():
        h_sc[...] = _rms(x_ref[...], gpre_ref[...]).astype(BF16)
        acc_sc[...] = jnp.zeros_like(acc_sc)

    u = jnp.maximum(jnp.dot(h_sc[...], w1_ref[...], preferred_element_type=F32), 0.0)
    acc_sc[...] += _mm(u * u, w2_ref[...])

    @pl.when(j == pl.num_programs(1) - 1)
    def _():
        o_ref[...] = x_ref[...] + _rms(acc_sc[...], gpost_ref[...])


def _ffn(x2, gpre, w1, w2, gpost, tm=512, tf=1024):
    t, d = x2.shape
    ff = w1.shape[1]
    return pl.pallas_call(
        _ffn_kernel,
        grid=(t // tm, ff // tf),
        in_specs=[
            pl.BlockSpec((tm, d), lambda i, j: (i, 0)),
            pl.BlockSpec((1, d), lambda i, j: (0, 0)),
            pl.BlockSpec((d, tf), lambda i, j: (0, j)),
            pl.BlockSpec((tf, d), lambda i, j: (j, 0)),
            pl.BlockSpec((1, d), lambda i, j: (0, 0)),
        ],
        out_specs=pl.BlockSpec((tm, d), lambda i, j: (i, 0)),
        out_shape=jax.ShapeDtypeStruct((t, d), F32),
        scratch_shapes=[pltpu.VMEM((tm, d), BF16), pltpu.VMEM((tm, d), F32)],
        compiler_params=_params("arbitrary", "arbitrary"),
    )(x2, gpre, w1, w2, gpost)


def _ple_kernel(x_ref, p_ref, wg_ref, wp_ref, g_ref, o_ref):
    x = x_ref[...]
    e = jax.nn.sigmoid(_mm(x, wg_ref[...])) * _mm(p_ref[...], wp_ref[...])
    o_ref[...] = x + _rms(e, g_ref[...])


def _ple(x2, p2, wg, wp, gain, tm=512):
    t, d = x2.shape
    dp = p2.shape[1]
    const = lambda i: (0, 0)
    row = lambda i: (i, 0)
    return pl.pallas_call(
        _ple_kernel,
        grid=(t // tm,),
        in_specs=[
            pl.BlockSpec((tm, d), row),
            pl.BlockSpec((tm, dp), row),
            pl.BlockSpec((d, d), const),
            pl.BlockSpec((dp, d), const),
            pl.BlockSpec((1, d), const),
        ],
        out_specs=pl.BlockSpec((tm, d), row),
        out_shape=jax.ShapeDtypeStruct((t, d), F32),
        compiler_params=_params("arbitrary"),
    )(x2, p2, wg, wp, gain)


def _pad_cols(w, n):
    return jnp.pad(w, ((0, 0), (0, n - w.shape[1])))


def _block_diag(blocks):
    n, d, e = blocks.shape
    eye = jnp.eye(n, dtype=blocks.dtype)
    return (eye[:, None, :, None] * blocks[:, :, None, :]).reshape(n * d, n * e)


def _mixers(z_parts, b, s, ml_gate_bias, ml_head_norm, lru, rw):
    zml, zlru, zrw, gcol, grow = z_parts
    L = CHUNK
    zml = zml.reshape(b, s, -1)
    zlru = zlru.reshape(b, s, -1)
    zrw = zrw.reshape(b, s, -1)
    gcol = gcol.reshape(b, s, -1)
    grow = grow.reshape(2 * ML_HEADS, b, s // L, L).transpose(1, 2, 0, 3)
    bias = ml_gate_bias.reshape(1, 2 * ML_HEADS)
    yml = _mlstm(zml, gcol, grow, _pad_cols(bias, GATE_PAD), bias.reshape(-1, 1),
                 ml_head_norm.reshape(1, -1))
    ylru = _rglru(zlru, *lru)
    yrw = _rwkv(zrw, *rw)
    return yml, ylru, yrw


def kernel(x, p, norm_mix_pre, norm_mix_post, norm_ffn_pre, norm_ffn_post, norm_ple, w_in, w_out, ml_gate_bias, ml_head_norm, lru_conv_w, lru_conv_b, lru_w_r, lru_b_r, lru_w_i, lru_b_i, lru_lambda, rw_mu, rw_w0, rw_w2, rw_a0, rw_a2, rw_g2, rw_k_k, rw_k_a, rw_r_k, rw_ln_w, rw_ln_b, ffn_w1, ffn_w2, ple_w_proj, ple_w_gate):
    b, s, d = x.shape
    depth = w_in.shape[0]
    t = b * s
    ml_in = 4 * ML_DIM + 2 * ML_HEADS
    lru_in = 2 * LRU_DIM
    head_id = jnp.arange(RW_DIM) // HD
    hsum = (head_id[:, None] == head_id[None, :]).astype(BF16)
    row = lambda v: v.reshape(1, -1)

    x2 = x.reshape(t, d)
    for l in range(depth):
        wi = w_in[l]
        w_gate = wi[:, 4 * ML_DIM:ml_in]
        z_parts = _in_proj(
            x2, row(norm_mix_pre[l]),
            wi[:, :4 * ML_DIM].astype(BF16),
            wi[:, ml_in:ml_in + lru_in].astype(BF16),
            _pad_cols(wi[:, ml_in + lru_in:], RW_IN_PAD).astype(BF16),
            _pad_cols(w_gate, GATE_PAD).astype(BF16),
            w_gate.T.astype(BF16))

        lru = (lru_conv_w[l], row(lru_conv_b[l]),
               jnp.concatenate([_block_diag(lru_w_r[l]), _block_diag(lru_w_i[l])], axis=1).astype(BF16),
               jnp.concatenate([row(lru_b_r[l]), row(lru_b_i[l])], axis=1),
               row(lru_lambda[l]))

        w_lora = jnp.zeros((RW_LORA_PAD, 3 * RW_DIM), F32)
        w_lora = w_lora.at[0:32, 0:RW_DIM].set(rw_w2[l])
        w_lora = w_lora.at[32:64, RW_DIM:2 * RW_DIM].set(rw_a2[l])
        w_lora = w_lora.at[64:RW_LORA, 2 * RW_DIM:].set(rw_g2[l])
        rw = (_pad_cols(row(rw_mu[l]), RW_IN_PAD), w_lora.astype(BF16), row(rw_w0[l]), row(rw_a0[l]),
              row(rw_k_k[l]), row(rw_k_a[l]), row(rw_r_k[l]), row(rw_ln_w[l]), row(rw_ln_b[l]), hsum)

        yml, ylru, yrw = _mixers(z_parts, b, s, ml_gate_bias[l], ml_head_norm[l], lru, rw)
        x2 = _out_proj(yml.reshape(t, -1), ylru.reshape(t, -1), yrw.reshape(t, -1), x2,
                       w_out[l].astype(BF16), row(norm_mix_post[l]))
        x2 = _ffn(x2, row(norm_ffn_pre[l]), ffn_w1[l].astype(BF16), ffn_w2[l].astype(BF16),
                  row(norm_ffn_post[l]))
        x2 = _ple(x2, p[l].reshape(t, -1), ple_w_gate[l].astype(BF16), ple_w_proj[l].astype(BF16),
                  row(norm_ple[l]))
    return x2.reshape(b, s, d)
```

```python
import functools

import jax
import jax.numpy as jnp
from jax import lax
from jax.experimental import pallas as pl
from jax.experimental.pallas import tpu as pltpu

F32 = jnp.float32
BF16 = jnp.bfloat16

EPS = 1e-6
HD = 64
ML_HEADS = 4
ML_DIM = ML_HEADS * HD
LRU_DIM = 256
LRU_C = 8.0
CONV_W = 4
RW_HEADS = 8
RW_DIM = RW_HEADS * HD
RW_LORA = 160
RW_LORA_PAD = 256
RW_IN_PAD = 3 * RW_DIM + RW_LORA_PAD
RW_GN_EPS = 64e-5
CHUNK = 64
GATE_PAD = 128

VMEM_LIMIT = 48 * 1024 * 1024

_NT = (((1,), (1,)), ((), ()))
_TN = (((0,), (0,)), ((), ()))


def _mm(a, b):
    return jnp.dot(a.astype(BF16), b.astype(BF16), preferred_element_type=F32)


def _mm_nt(a, b):
    return lax.dot_general(a.astype(BF16), b.astype(BF16), _NT, preferred_element_type=F32)


def _mm_tn(a, b):
    return lax.dot_general(a.astype(BF16), b.astype(BF16), _TN, preferred_element_type=F32)


def _mm_f32(a, b):
    return jnp.dot(a, b, preferred_element_type=F32, precision=lax.Precision.HIGHEST)


def _rms(x, gain):
    return x * lax.rsqrt(jnp.mean(x * x, axis=-1, keepdims=True) + EPS) * gain


def _softplus(x):
    return jnp.maximum(x, 0.0) + jnp.log(1.0 + jnp.exp(-jnp.abs(x)))


def _log_sigmoid(x):
    return jnp.minimum(x, 0.0) - jnp.log(1.0 + jnp.exp(-jnp.abs(x)))


def _iota2(shape, dim):
    return lax.broadcasted_iota(jnp.int32, shape, dim)


def _params(*sem):
    return pltpu.CompilerParams(dimension_semantics=sem, vmem_limit_bytes=VMEM_LIMIT)


def _bd(y, left):
    return jnp.concatenate([jnp.where(left, y, 0.0), jnp.where(left, 0.0, y)], axis=0)


def _unbd(z, left):
    return jnp.where(left, z[0:HD, :], z[HD:2 * HD, :])


def _in_proj_kernel(x_ref, g_ref, wml_ref, wlru_ref, wrw_ref, wg_ref, wgt_ref,
                    zml_ref, zlru_ref, zrw_ref, gcol_ref, grow_ref):
    hb = _rms(x_ref[...], g_ref[...]).astype(BF16)
    zml_ref[...] = jnp.dot(hb, wml_ref[...], preferred_element_type=F32)
    zlru_ref[...] = jnp.dot(hb, wlru_ref[...], preferred_element_type=F32)
    zrw_ref[...] = jnp.dot(hb, wrw_ref[...], preferred_element_type=F32)
    gcol_ref[...] = jnp.dot(hb, wg_ref[...], preferred_element_type=F32)
    grow_ref[...] = lax.dot_general(wgt_ref[...], hb, _NT, preferred_element_type=F32)


def _in_proj(x2, gain, wml, wlru, wrw, wg, wgt, tm=256):
    t, d = x2.shape
    const = lambda i: (0, 0)
    row = lambda i: (i, 0)
    outs = (
        jax.ShapeDtypeStruct((t, wml.shape[1]), F32),
        jax.ShapeDtypeStruct((t, wlru.shape[1]), F32),
        jax.ShapeDtypeStruct((t, wrw.shape[1]), F32),
        jax.ShapeDtypeStruct((t, GATE_PAD), F32),
        jax.ShapeDtypeStruct((wgt.shape[0], t), F32),
    )
    return pl.pallas_call(
        _in_proj_kernel,
        grid=(t // tm,),
        in_specs=[
            pl.BlockSpec((tm, d), row),
            pl.BlockSpec((1, d), const),
            pl.BlockSpec(wml.shape, const),
            pl.BlockSpec(wlru.shape, const),
            pl.BlockSpec(wrw.shape, const),
            pl.BlockSpec(wg.shape, const),
            pl.BlockSpec(wgt.shape, const),
        ],
        out_specs=(
            pl.BlockSpec((tm, wml.shape[1]), row),
            pl.BlockSpec((tm, wlru.shape[1]), row),
            pl.BlockSpec((tm, wrw.shape[1]), row),
            pl.BlockSpec((tm, GATE_PAD), row),
            pl.BlockSpec((wgt.shape[0], tm), lambda i: (0, i)),
        ),
        out_shape=outs,
        compiler_params=_params("arbitrary"),
    )(x2, gain, wml, wlru, wrw, wg, wgt)


def _mlstm_kernel(z_ref, gcol_ref, grow_ref, bcol_ref, brow_ref, hn_ref, y_ref, st_sc, m_sc):
    L = CHUNK
    ts = z_ref.shape[0]
    nc = ts // L
    npair = ML_HEADS // 2
    pw = 2 * HD

    @pl.when(pl.program_id(1) == 0)
    def _():
        st_sc[...] = jnp.zeros_like(st_sc)
        m_sc[...] = jnp.zeros_like(m_sc)

    rt_i = _iota2((ts, ts), 0)
    ct_i = _iota2((ts, ts), 1)
    same = (rt_i // L) == (ct_i // L)
    gc = gcol_ref[...] + bcol_ref[...]
    f_col = _mm_f32(((rt_i >= ct_i) & same).astype(F32), _log_sigmoid(gc))
    gr = grow_ref[...] + brow_ref[...]
    f_row = _mm_f32(_log_sigmoid(gr), ((rt_i <= ct_i) & same).astype(F32))
    g_rows = f_row[ML_HEADS:2 * ML_HEADS, :] - gr[0:ML_HEADS, :]
    g_up = pltpu.roll(g_rows, HD, 1)
    g_dn = pltpu.roll(g_rows, ts - HD, 1)

    t_i = _iota2((L, pw), 0)
    lane = _iota2((L, pw), 1)
    left = lane < HD
    left1 = left[0:1, :]
    causal = t_i >= (lane & (HD - 1))
    r2 = _iota2((pw, 2 * pw), 0)
    c2 = _iota2((pw, 2 * pw), 1)
    diag2 = (r2 // HD) == ((c2 // HD) % 2)
    top2 = r2 < HD
    ones_bd = diag2[:, 0:pw].astype(F32)
    ones_l = jnp.ones((L, pw), F32)
    scale = HD ** -0.5

    def pair_cols(arr, c, j0, j1):
        rows = arr[c * L:(c + 1) * L, :]
        return jnp.where(left, rows[:, j0:j0 + 1], rows[:, j1:j1 + 1])

    chains = [(p, c) for p in range(npair) for c in range(nc)]

    pre = []
    for p, c in chains:
        h0, h1 = 2 * p, 2 * p + 1
        rows = slice(c * L, (c + 1) * L)
        q = z_ref[rows, p * pw:(p + 1) * pw] * scale
        k = z_ref[rows, ML_DIM + p * pw:ML_DIM + (p + 1) * pw]
        v = z_ref[rows, 2 * ML_DIM + p * pw:2 * ML_DIM + (p + 1) * pw]
        fc = pair_cols(f_col, c, ML_HEADS + h0, ML_HEADS + h1)
        ic = pair_cols(gc, c, h0, h1)
        w0 = (c // 2) * pw
        if c % 2 == 0:
            g_pair = jnp.where(left1, g_rows[h0:h0 + 1, w0:w0 + pw], g_up[h1:h1 + 1, w0:w0 + pw])
        else:
            g_pair = jnp.where(left1, g_dn[h0:h0 + 1, w0:w0 + pw], g_rows[h1:h1 + 1, w0:w0 + pw])
        log_d = jnp.where(causal, fc - g_pair, -jnp.inf)
        d_max = jnp.where(left,
                          jnp.max(jnp.where(left, log_d, -jnp.inf), axis=-1, keepdims=True),
                          jnp.max(jnp.where(left, -jnp.inf, log_d), axis=-1, keepdims=True))
        s_raw = _mm_nt(q, _bd(k, left))
        f_tot = fc[L - 1:L, :]
        lw = f_tot - fc + ic
        m_loc = jnp.max(lw, axis=0, keepdims=True)
        kw = k * jnp.exp(lw - m_loc)
        new = _mm_tn(kw, jnp.concatenate([v, ones_l], axis=1))
        top = jnp.concatenate([_bd(v, left), ones_bd], axis=1)
        pre.append((q, fc, log_d, d_max, s_raw, f_tot, m_loc, new, top))

    lhs, rhs, m_rows = [], [], []
    for p in range(npair):
        state = st_sc[p]
        m_in = m_sc[p:p + 1, :]
        for c in range(nc):
            q, fc, log_d, d_max, s_raw, f_tot, m_loc, new, top = pre[p * nc + c]
            m_inter = fc + m_in
            m_row = jnp.maximum(m_inter, d_max)
            sd = s_raw * jnp.exp(log_d - m_row)
            lhs.append(jnp.concatenate([sd, jnp.exp(m_inter - m_row) * q], axis=1))
            rhs.append(jnp.concatenate([top, state], axis=0))
            m_rows.append(m_row)
            m_new = jnp.maximum(f_tot + m_in, m_loc)
            a = jnp.exp(f_tot + m_in - m_new)
            b = jnp.exp(m_loc - m_new)
            a_col = jnp.where(top2, a[:, 0:1], a[:, HD:HD + 1])
            b_col = jnp.where(top2, b[:, 0:1], b[:, HD:HD + 1])
            state = a_col * state + jnp.where(diag2, b_col * new, 0.0)
            m_in = m_new
        st_sc[p] = state
        m_sc[p:p + 1, :] = m_in

    for i, (p, c) in enumerate(chains):
        rows = slice(c * L, (c + 1) * L)
        nd = _mm(lhs[i], rhs[i])
        hh = nd[:, 0:pw] / jnp.maximum(jnp.abs(nd[:, pw:]), jnp.exp(-m_rows[i]))
        ms = _mm(hh * hh, ones_bd) * (1.0 / HD)
        o = z_ref[rows, 3 * ML_DIM + p * pw:3 * ML_DIM + (p + 1) * pw]
        y_ref[rows, p * pw:(p + 1) * pw] = ((hh * lax.rsqrt(ms + EPS) * hn_ref[:, p * pw:(p + 1) * pw])
                                            * jax.nn.sigmoid(o))


def _mlstm(zml, gcol, grow, bcol, brow, head_norm, ts=256):
    b, s, _ = zml.shape
    ts = min(ts, s)
    nblk = s // ts
    const = lambda i, j: (0, 0)
    return pl.pallas_call(
        _mlstm_kernel,
        grid=(b, nblk),
        in_specs=[
            pl.BlockSpec((None, ts, 4 * ML_DIM), lambda i, j: (i, j, 0)),
            pl.BlockSpec((None, ts, GATE_PAD), lambda i, j: (i, j, 0)),
            pl.BlockSpec((2 * ML_HEADS, ts), lambda i, j: (0, i * nblk + j)),
            pl.BlockSpec((1, GATE_PAD), const),
            pl.BlockSpec((2 * ML_HEADS, 1), const),
            pl.BlockSpec((1, ML_DIM), const),
        ],
        out_specs=pl.BlockSpec((None, ts, ML_DIM), lambda i, j: (i, j, 0)),
        out_shape=jax.ShapeDtypeStruct((b, s, ML_DIM), F32),
        scratch_shapes=[
            pltpu.VMEM((ML_HEADS // 2, 2 * HD, 4 * HD), F32),
            pltpu.VMEM((8, 2 * HD), F32),
        ],
        compiler_params=_params("arbitrary", "arbitrary"),
    )(zml, gcol, grow, bcol, brow, head_norm)


def _shift_rows(x, d, fill, row):
    return jnp.where(row >= d, pltpu.roll(x, d, 0), fill)


def _lru_kernel(z_ref, cw_ref, cb_ref, wg_ref, bg_ref, lam_ref, y_ref, xbuf, h_sc):
    ts = z_ref.shape[0]

    @pl.when(pl.program_id(1) == 0)
    def _():
        xbuf[0:8, :] = jnp.zeros((8, LRU_DIM), F32)
        h_sc[...] = jnp.zeros_like(h_sc)

    xb = z_ref[:, 0:LRU_DIM]
    gb = z_ref[:, LRU_DIM:2 * LRU_DIM]
    xbuf[8:8 + ts, :] = xb
    xc = cb_ref[...] + cw_ref[CONV_W - 1:CONV_W, :] * xb
    for j in range(1, CONV_W):
        xc = xc + cw_ref[CONV_W - 1 - j:CONV_W - j, :] * xbuf[8 - j:8 - j + ts, :]
    xbuf[0:8, :] = xb[ts - 8:ts, :]

    gates = _mm(xc, wg_ref[...]) + bg_ref[...]
    r = jax.nn.sigmoid(gates[:, 0:LRU_DIM])
    i = jax.nn.sigmoid(gates[:, LRU_DIM:2 * LRU_DIM])
    log_a = -LRU_C * r * _softplus(-lam_ref[...])
    a = jnp.exp(log_a)
    u = jnp.sqrt(-jnp.tanh(log_a) * (a * a + 1.0)) * (i * xc)

    row = _iota2((ts, LRU_DIM), 0)
    acc_a, acc_h = a, u
    d = 1
    while d < ts:
        acc_h = acc_a * _shift_rows(acc_h, d, 0.0, row) + acc_h
        acc_a = acc_a * _shift_rows(acc_a, d, 1.0, row)
        d *= 2
    hfull = acc_a * h_sc[0:1, :] + acc_h
    h_sc[...] = jnp.broadcast_to(hfull[ts - 1:ts, :], h_sc.shape)

    gelu = 0.5 * gb * (1.0 + jnp.tanh(0.7978845608028654 * (gb + 0.044715 * (gb * gb * gb))))
    y_ref[...] = hfull * gelu


def _rglru(zlru, conv_w, conv_b, wg, bg, lam, ts=512):
    b, s, _ = zlru.shape
    ts = min(ts, s)
    const = lambda i, j: (0, 0)
    return pl.pallas_call(
        _lru_kernel,
        grid=(b, s // ts),
        in_specs=[
            pl.BlockSpec((None, ts, 2 * LRU_DIM), lambda i, j: (i, j, 0)),
            pl.BlockSpec((CONV_W, LRU_DIM), const),
            pl.BlockSpec((1, LRU_DIM), const),
            pl.BlockSpec((LRU_DIM, 2 * LRU_DIM), const),
            pl.BlockSpec((1, 2 * LRU_DIM), const),
            pl.BlockSpec((1, LRU_DIM), const),
        ],
        out_specs=pl.BlockSpec((None, ts, LRU_DIM), lambda i, j: (i, j, 0)),
        out_shape=jax.ShapeDtypeStruct((b, s, LRU_DIM), F32),
        scratch_shapes=[
            pltpu.VMEM((ts + 8, LRU_DIM), F32),
            pltpu.VMEM((8, LRU_DIM), F32),
        ],
        compiler_params=_params("arbitrary", "arbitrary"),
    )(zlru, conv_w, conv_b, wg, bg, lam)


def _rwkv_kernel(z_ref, mu_ref, wl_ref, w0_ref, a0_ref, kk_ref, ka_ref, rk_ref, lnw_ref, lnb_ref,
                 hsum_ref, y_ref, prev_sc, p_sc, o_sc):
    L = CHUNK
    ts = z_ref.shape[0]
    nc = ts // L
    npair = RW_HEADS // 2

    @pl.when(pl.program_id(1) == 0)
    def _():
        prev_sc[...] = jnp.zeros_like(prev_sc)
        p_sc[...] = jnp.zeros_like(p_sc)

    z = z_ref[...]
    row = _iota2(z.shape, 0)
    zprev = jnp.where(row == 0, prev_sc[0:1, :], pltpu.roll(z, 1, 0))
    prev_sc[...] = jnp.broadcast_to(z[ts - 1:ts, :], prev_sc.shape)
    z = z + (zprev - z) * mu_ref[...]

    r = z[:, 0:RW_DIM]
    k = z[:, RW_DIM:2 * RW_DIM]
    v = z[:, 2 * RW_DIM:3 * RW_DIM]
    lo = z[:, 3 * RW_DIM:3 * RW_DIM + RW_LORA_PAD]
    lane = _iota2(lo.shape, 1)
    lo = jnp.where(lane < 32, jnp.tanh(lo), jnp.where(lane < 64, lo, jax.nn.sigmoid(lo)))
    lora = _mm(lo, wl_ref[...])
    w = -_softplus(-(w0_ref[...] + lora[:, 0:RW_DIM])) - 0.5
    log_w = -jnp.exp(w)
    a = jax.nn.sigmoid(a0_ref[...] + lora[:, RW_DIM:2 * RW_DIM])
    g = lora[:, 2 * RW_DIM:3 * RW_DIM]

    hsum = hsum_ref[...]
    kk = k * kk_ref[...]
    kk = kk / jnp.maximum(jnp.sqrt(_mm(kk * kk, hsum)), 1e-12)
    k = k * (1.0 + (a - 1.0) * ka_ref[...])
    bonus = _mm(r * k * rk_ref[...], hsum) * v
    b = kk * a

    rt_i = _iota2((ts, ts), 0)
    ct_i = _iota2((ts, ts), 1)
    tri = ((rt_i >= ct_i) & ((rt_i // L) == (ct_i // L))).astype(F32)
    cum = _mm_f32(tri, log_w)
    g_inv = jnp.exp(-cum)
    r_t = r * jnp.exp(cum)
    k_t = k * g_inv
    b_t = b * g_inv
    kp_t = kk * jnp.exp(cum - log_w)

    t_i = _iota2((L, 2 * HD), 0)
    lane = _iota2((L, 2 * HD), 1)
    s_i = lane & (HD - 1)
    left = lane < HD
    lower = t_i >= s_i
    strict = t_i > s_i
    eye = (t_i == s_i).astype(F32)
    levels = []
    bs = 1
    while bs < L:
        shift = bs.bit_length() - 1
        x, y = t_i >> shift, s_i >> shift
        levels.append(((x ^ y) == 1) & (x > y))
        bs *= 2

    chains = [(c, p) for c in range(nc) for p in range(npair)]
    cut = lambda arr, c, p: arr[c * L:(c + 1) * L, p * 2 * HD:(p + 1) * 2 * HD]

    rt, kpt, vv, qk, qb, ak, ab = [], [], [], [], [], [], []
    for c, p in chains:
        rt.append(cut(r_t, c, p))
        kpt.append(cut(kp_t, c, p))
        vv.append(cut(v, c, p))
        rhs = jnp.concatenate([_bd(cut(k_t, c, p), left), _bd(cut(b_t, c, p), left)], axis=0)
        sc = _mm_nt(jnp.concatenate([rt[-1], kpt[-1]], axis=0), rhs)
        qk.append(jnp.where(lower, sc[0:L, 0:2 * HD], 0.0))
        qb.append(jnp.where(lower, sc[0:L, 2 * HD:], 0.0))
        ak.append(jnp.where(strict, sc[L:, 0:2 * HD], 0.0))
        ab.append(jnp.where(strict, sc[L:, 2 * HD:], 0.0))

    tinv = [eye - jnp.where(levels[0], a, 0.0) for a in ab]
    for lvl in levels[1:]:
        xs = [_mm(jnp.where(lvl, a, 0.0), _bd(t, left)) for a, t in zip(ab, tinv)]
        tinv = [t - _mm(t, _bd(x, left)) for t, x in zip(tinv, xs)]

    qa = [_mm(jnp.concatenate([q, a], axis=0), _bd(v_, left)) for q, a, v_ in zip(qk, ak, vv)]
    ww = [_mm(t, jnp.concatenate([_bd(kp, left), _bd(q[L:], left)], axis=1))
          for t, kp, q in zip(tinv, kpt, qa)]
    qw = [_mm(q, jnp.concatenate([_bd(w[:, 0:2 * HD], left), _bd(w[:, 2 * HD:], left)], axis=1))
          for q, w in zip(qb, ww)]
    r_hat = [r_ - q[:, 0:2 * HD] for r_, q in zip(rt, qw)]
    o_loc = [q[0:L] - w[:, 2 * HD:] for q, w in zip(qa, qw)]

    m_st, n_st = [], []
    for i, (c, p) in enumerate(chains):
        cum_c = cut(cum, c, p)
        cum_end = cum_c[L - 1:L, :]
        g_end = jnp.exp(cum_end - cum_c)
        g1 = _mm_tn(cut(b, c, p) * g_end, ww[i])
        g2 = _mm_tn(cut(k, c, p) * g_end, vv[i])
        m_st.append(eye * jnp.exp(cum_end) - _unbd(g1[:, 0:2 * HD], left))
        n_st.append(_unbd(g2, left) - _unbd(g1[:, 2 * HD:], left))

    state = [p_sc[p] for p in range(npair)]
    for i, (c, p) in enumerate(chains):
        op = _mm(jnp.concatenate([r_hat[i], m_st[i]], axis=0), _bd(state[p], left))
        o_sc[c * L:(c + 1) * L, p * 2 * HD:(p + 1) * 2 * HD] = op[0:L] + o_loc[i]
        state[p] = op[L:] + n_st[i]
    for p in range(npair):
        p_sc[p] = state[p]
    o = o_sc[...]

    mean = _mm(o, hsum) * (1.0 / HD)
    dev = o - mean
    var = _mm(dev * dev, hsum) * (1.0 / HD)
    o = dev * lax.rsqrt(var + RW_GN_EPS) * lnw_ref[...] + lnb_ref[...]
    y_ref[...] = (o + bonus) * g


def _rwkv(zrw, mu, wl, w0, a0, k_k, k_a, r_k, ln_w, ln_b, hsum, ts=256):
    b, s, _ = zrw.shape
    ts = min(ts, s)
    const = lambda i, j: (0, 0)
    vec = pl.BlockSpec((1, RW_DIM), const)
    return pl.pallas_call(
        _rwkv_kernel,
        grid=(b, s // ts),
        in_specs=[
            pl.BlockSpec((None, ts, RW_IN_PAD), lambda i, j: (i, j, 0)),
            pl.BlockSpec((1, RW_IN_PAD), const),
            pl.BlockSpec((RW_LORA_PAD, 3 * RW_DIM), const),
            vec, vec, vec, vec, vec, vec, vec,
            pl.BlockSpec((RW_DIM, RW_DIM), const),
        ],
        out_specs=pl.BlockSpec((None, ts, RW_DIM), lambda i, j: (i, j, 0)),
        out_shape=jax.ShapeDtypeStruct((b, s, RW_DIM), F32),
        scratch_shapes=[
            pltpu.VMEM((8, RW_IN_PAD), F32),
            pltpu.VMEM((RW_HEADS // 2, HD, 2 * HD), F32),
            pltpu.VMEM((ts, RW_DIM), F32),
        ],
        compiler_params=_params("arbitrary", "arbitrary"),
    )(zrw, mu, wl, w0, a0, k_k, k_a, r_k, ln_w, ln_b, hsum)


def _out_proj_kernel(yml_ref, ylru_ref, yrw_ref, x_ref, w_ref, g_ref, o_ref):
    mix = (_mm(yml_ref[...], w_ref[0:ML_DIM, :])
           + _mm(ylru_ref[...], w_ref[ML_DIM:ML_DIM + LRU_DIM, :])
           + _mm(yrw_ref[...], w_ref[ML_DIM + LRU_DIM:, :]))
    o_ref[...] = x_ref[...] + _rms(mix, g_ref[...])


def _out_proj(yml, ylru, yrw, x2, w, gain, tm=512):
    t, d = x2.shape
    const = lambda i: (0, 0)
    row = lambda i: (i, 0)
    return pl.pallas_call(
        _out_proj_kernel,
        grid=(t // tm,),
        in_specs=[
            pl.BlockSpec((tm, ML_DIM), row),
            pl.BlockSpec((tm, LRU_DIM), row),
            pl.BlockSpec((tm, RW_DIM), row),
            pl.BlockSpec((tm, d), row),
            pl.BlockSpec(w.shape, const),
            pl.BlockSpec((1, d), const),
        ],
        out_specs=pl.BlockSpec((tm, d), row),
        out_shape=jax.ShapeDtypeStruct((t, d), F32),
        compiler_params=_params("arbitrary"),
    )(yml, ylru, yrw, x2, w, gain)


def _ffn_kernel(x_ref, gpre_ref, w1_ref, w2_ref, gpost_ref, o_ref, h_sc, acc_sc):
    j = pl.program_id(1)

    @pl.when(j == 0)
    def _():
        h_sc[...] = _rms(x_ref[...], gpre_ref[...]).astype(BF16)
        acc_sc[...] = jnp.zeros_like(acc_sc)

    u = jnp.maximum(jnp.dot(h_sc[...], w1_ref[...], preferred_element_type=F32), 0.0)
    acc_sc[...] += _mm(u * u, w2_ref[...])

    @pl.when(j == pl.num_programs(1) - 1)
    def _():
        o_ref[...] = x_ref[...] + _rms(acc_sc[...], gpost_ref[...])


def _ffn(x2, gpre, w1, w2, gpost, tm=512, tf=1024):
    t, d = x2.shape
    ff = w1.shape[1]
    return pl.pallas_call(
        _ffn_kernel,
        grid=(t // tm, ff // tf),
        in_specs=[
            pl.BlockSpec((tm, d), lambda i, j: (i, 0)),
            pl.BlockSpec((1, d), lambda i, j: (0, 0)),
            pl.BlockSpec((d, tf), lambda i, j: (0, j)),
            pl.BlockSpec((tf, d), lambda i, j: (j, 0)),
            pl.BlockSpec((1, d), lambda i, j: (0, 0)),
        ],
        out_specs=pl.BlockSpec((tm, d), lambda i, j: (i, 0)),
        out_shape=jax.ShapeDtypeStruct((t, d), F32),
        scratch_shapes=[pltpu.VMEM((tm, d), BF16), pltpu.VMEM((tm, d), F32)],
        compiler_params=_params("arbitrary", "arbitrary"),
    )(x2, gpre, w1, w2, gpost)


def _ple_kernel(x_ref, p_ref, wg_ref, wp_ref, g_ref, o_ref):
    x = x_ref[...]
    e = jax.nn.sigmoid(_mm(x, wg_ref[...])) * _mm(p_ref[...], wp_ref[...])
    o_ref[...] = x + _rms(e, g_ref[...])


def _ple(x2, p2, wg, wp, gain, tm=512):
    t, d = x2.shape
    dp = p2.shape[1]
    const = lambda i: (0, 0)
    row = lambda i: (i, 0)
    return pl.pallas_call(
        _ple_kernel,
        grid=(t // tm,),
        in_specs=[
            pl.BlockSpec((tm, d), row),
            pl.BlockSpec((tm, dp), row),
            pl.BlockSpec((d, d), const),
            pl.BlockSpec((dp, d), const),
            pl.BlockSpec((1, d), const),
        ],
        out_specs=pl.BlockSpec((tm, d), row),
        out_shape=jax.ShapeDtypeStruct((t, d), F32),
        compiler_params=_params("arbitrary"),
    )(x2, p2, wg, wp, gain)


def _pad_cols(w, n):
    return jnp.pad(w, ((0, 0), (0, n - w.shape[1])))


def _block_diag(blocks):
    n, d, e = blocks.shape
    eye = jnp.eye(n, dtype=blocks.dtype)
    return (eye[:, None, :, None] * blocks[:, :, None, :]).reshape(n * d, n * e)


def _mixers(z_parts, b, s, ml_gate_bias, ml_head_norm, lru, rw):
    zml, zlru, zrw, gcol, grow = z_parts
    zml = zml.reshape(b, s, -1)
    zlru = zlru.reshape(b, s, -1)
    zrw = zrw.reshape(b, s, -1)
    gcol = gcol.reshape(b, s, -1)
    bias = ml_gate_bias.reshape(1, 2 * ML_HEADS)
    yml = _mlstm(zml, gcol, grow, _pad_cols(bias, GATE_PAD), bias.reshape(-1, 1),
                 ml_head_norm.reshape(1, -1))
    ylru = _rglru(zlru, *lru)
    yrw = _rwkv(zrw, *rw)
    return yml, ylru, yrw


def kernel(x, p, norm_mix_pre, norm_mix_post, norm_ffn_pre, norm_ffn_post, norm_ple, w_in, w_out, ml_gate_bias, ml_head_norm, lru_conv_w, lru_conv_b, lru_w_r, lru_b_r, lru_w_i, lru_b_i, lru_lambda, rw_mu, rw_w0, rw_w2, rw_a0, rw_a2, rw_g2, rw_k_k, rw_k_a, rw_r_k, rw_ln_w, rw_ln_b, ffn_w1, ffn_w2, ple_w_proj, ple_w_gate):
    b, s, d = x.shape
    depth = w_in.shape[0]
    t = b * s
    ml_in = 4 * ML_DIM + 2 * ML_HEADS
    lru_in = 2 * LRU_DIM
    head_id = jnp.arange(RW_DIM) // HD
    hsum = (head_id[:, None] == head_id[None, :]).astype(BF16)
    row = lambda v: v.reshape(1, -1)

    x2 = x.reshape(t, d)
    for l in range(depth):
        wi = w_in[l]
        w_gate = wi[:, 4 * ML_DIM:ml_in]
        z_parts = _in_proj(
            x2, row(norm_mix_pre[l]),
            wi[:, :4 * ML_DIM].astype(BF16),
            wi[:, ml_in:ml_in + lru_in].astype(BF16),
            _pad_cols(wi[:, ml_in + lru_in:], RW_IN_PAD).astype(BF16),
            _pad_cols(w_gate, GATE_PAD).astype(BF16),
            w_gate.T.astype(BF16))

        lru = (lru_conv_w[l], row(lru_conv_b[l]),
               jnp.concatenate([_block_diag(lru_w_r[l]), _block_diag(lru_w_i[l])], axis=1).astype(BF16),
               jnp.concatenate([row(lru_b_r[l]), row(lru_b_i[l])], axis=1),
               row(lru_lambda[l]))

        w_lora = jnp.zeros((RW_LORA_PAD, 3 * RW_DIM), F32)
        w_lora = w_lora.at[0:32, 0:RW_DIM].set(rw_w2[l])
        w_lora = w_lora.at[32:64, RW_DIM:2 * RW_DIM].set(rw_a2[l])
        w_lora = w_lora.at[64:RW_LORA, 2 * RW_DIM:].set(rw_g2[l])
        rw = (_pad_cols(row(rw_mu[l]), RW_IN_PAD), w_lora.astype(BF16), row(rw_w0[l]), row(rw_a0[l]),
              row(rw_k_k[l]), row(rw_k_a[l]), row(rw_r_k[l]), row(rw_ln_w[l]), row(rw_ln_b[l]), hsum)

        yml, ylru, yrw = _mixers(z_parts, b, s, ml_gate_bias[l], ml_head_norm[l], lru, rw)
        x2 = _out_proj(yml.reshape(t, -1), ylru.reshape(t, -1), yrw.reshape(t, -1), x2,
                       w_out[l].astype(BF16), row(norm_mix_post[l]))
        x2 = _ffn(x2, row(norm_ffn_pre[l]), ffn_w1[l].astype(BF16), ffn_w2[l].astype(BF16),
                  row(norm_ffn_post[l]))
        x2 = _ple(x2, p[l].reshape(t, -1), ple_w_gate[l].astype(BF16), ple_w_proj[l].astype(BF16),
                  row(norm_ple[l]))
    return x2.reshape(b, s, d)
```

```python
import functools

import jax
import jax.numpy as jnp
from jax import lax
from jax.experimental import pallas as pl
from jax.experimental.pallas import tpu as pltpu

F32 = jnp.float32
BF16 = jnp.bfloat16

EPS = 1e-6
HD = 64
ML_HEADS = 4
ML_DIM = ML_HEADS * HD
LRU_DIM = 256
LRU_C = 8.0
CONV_W = 4
RW_HEADS = 8
RW_DIM = RW_HEADS * HD
RW_LORA = 160
RW_LORA_PAD = 256
RW_IN_PAD = 3 * RW_DIM + RW_LORA_PAD
RW_GN_EPS = 64e-5
CHUNK = 64
GATE_PAD = 128

VMEM_LIMIT = 48 * 1024 * 1024

_NT = (((1,), (1,)), ((), ()))
_TN = (((0,), (0,)), ((), ()))


def _mm(a, b):
    return jnp.dot(a.astype(BF16), b.astype(BF16), preferred_element_type=F32)


def _mm_nt(a, b):
    return lax.dot_general(a.astype(BF16), b.astype(BF16), _NT, preferred_element_type=F32)


def _mm_tn(a, b):
    return lax.dot_general(a.astype(BF16), b.astype(BF16), _TN, preferred_element_type=F32)


def _mm_f32(a, b):
    return jnp.dot(a, b, preferred_element_type=F32, precision=lax.Precision.HIGHEST)


def _rms(x, gain):
    return x * lax.rsqrt(jnp.mean(x * x, axis=-1, keepdims=True) + EPS) * gain


def _softplus(x):
    return jnp.maximum(x, 0.0) + jnp.log(1.0 + jnp.exp(-jnp.abs(x)))


def _log_sigmoid(x):
    return jnp.minimum(x, 0.0) - jnp.log(1.0 + jnp.exp(-jnp.abs(x)))


def _iota2(shape, dim):
    return lax.broadcasted_iota(jnp.int32, shape, dim)


def _params(*sem):
    return pltpu.CompilerParams(dimension_semantics=sem, vmem_limit_bytes=VMEM_LIMIT)


def _bd(y, left):
    return jnp.concatenate([jnp.where(left, y, 0.0), jnp.where(left, 0.0, y)], axis=0)


def _unbd(z, left):
    return jnp.where(left, z[0:HD, :], z[HD:2 * HD, :])


def _in_proj_kernel(x_ref, g_ref, wml_ref, wlru_ref, wrw_ref, wg_ref, wgt_ref,
                    zml_ref, zlru_ref, zrw_ref, gcol_ref, grow_ref):
    hb = _rms(x_ref[...], g_ref[...]).astype(BF16)
    zml_ref[...] = jnp.dot(hb, wml_ref[...], preferred_element_type=F32)
    zlru_ref[...] = jnp.dot(hb, wlru_ref[...], preferred_element_type=F32)
    zrw_ref[...] = jnp.dot(hb, wrw_ref[...], preferred_element_type=F32)
    gcol_ref[...] = jnp.dot(hb, wg_ref[...], preferred_element_type=F32)
    grow_ref[...] = lax.dot_general(wgt_ref[...], hb, _NT, preferred_element_type=F32)


def _in_proj(x2, gain, wml, wlru, wrw, wg, wgt, tm=512):
    t, d = x2.shape
    const = lambda i: (0, 0)
    row = lambda i: (i, 0)
    outs = (
        jax.ShapeDtypeStruct((t, wml.shape[1]), F32),
        jax.ShapeDtypeStruct((t, wlru.shape[1]), F32),
        jax.ShapeDtypeStruct((t, wrw.shape[1]), F32),
        jax.ShapeDtypeStruct((t, GATE_PAD), F32),
        jax.ShapeDtypeStruct((wgt.shape[0], t), F32),
    )
    return pl.pallas_call(
        _in_proj_kernel,
        grid=(t // tm,),
        in_specs=[
            pl.BlockSpec((tm, d), row),
            pl.BlockSpec((1, d), const),
            pl.BlockSpec(wml.shape, const),
            pl.BlockSpec(wlru.shape, const),
            pl.BlockSpec(wrw.shape, const),
            pl.BlockSpec(wg.shape, const),
            pl.BlockSpec(wgt.shape, const),
        ],
        out_specs=(
            pl.BlockSpec((tm, wml.shape[1]), row),
            pl.BlockSpec((tm, wlru.shape[1]), row),
            pl.BlockSpec((tm, wrw.shape[1]), row),
            pl.BlockSpec((tm, GATE_PAD), row),
            pl.BlockSpec((wgt.shape[0], tm), lambda i: (0, i)),
        ),
        out_shape=outs,
        compiler_params=_params("arbitrary"),
    )(x2, gain, wml, wlru, wrw, wg, wgt)


def _mlstm_kernel(z_ref, gcol_ref, grow_ref, bcol_ref, brow_ref, hn_ref, y_ref, st_sc, m_sc):
    L = CHUNK
    ts = z_ref.shape[0]
    nc = ts // L
    npair = ML_HEADS // 2
    pw = 2 * HD

    @pl.when(pl.program_id(1) == 0)
    def _():
        st_sc[...] = jnp.zeros_like(st_sc)
        m_sc[...] = jnp.zeros_like(m_sc)

    rt_i = _iota2((ts, ts), 0)
    ct_i = _iota2((ts, ts), 1)
    same = (rt_i // L) == (ct_i // L)
    gc = gcol_ref[...] + bcol_ref[...]
    f_col = _mm_f32(((rt_i >= ct_i) & same).astype(F32), _log_sigmoid(gc))
    gr = grow_ref[...] + brow_ref[...]
    f_row = _mm_f32(_log_sigmoid(gr), ((rt_i <= ct_i) & same).astype(F32))
    g_rows = f_row[ML_HEADS:2 * ML_HEADS, :] - gr[0:ML_HEADS, :]
    g_up = pltpu.roll(g_rows, HD, 1)
    g_dn = pltpu.roll(g_rows, ts - HD, 1)

    t_i = _iota2((L, pw), 0)
    lane = _iota2((L, pw), 1)
    left = lane < HD
    left1 = left[0:1, :]
    causal = t_i >= (lane & (HD - 1))
    r2 = _iota2((pw, 2 * pw), 0)
    c2 = _iota2((pw, 2 * pw), 1)
    diag2 = (r2 // HD) == ((c2 // HD) % 2)
    top2 = r2 < HD
    ones_bd = diag2[:, 0:pw].astype(F32)
    ones_l = jnp.ones((L, pw), F32)
    scale = HD ** -0.5

    def pair_cols(arr, c, j0, j1):
        rows = arr[c * L:(c + 1) * L, :]
        return jnp.where(left, rows[:, j0:j0 + 1], rows[:, j1:j1 + 1])

    chains = [(p, c) for p in range(npair) for c in range(nc)]

    pre = []
    for p, c in chains:
        h0, h1 = 2 * p, 2 * p + 1
        rows = slice(c * L, (c + 1) * L)
        q = z_ref[rows, p * pw:(p + 1) * pw] * scale
        k = z_ref[rows, ML_DIM + p * pw:ML_DIM + (p + 1) * pw]
        v = z_ref[rows, 2 * ML_DIM + p * pw:2 * ML_DIM + (p + 1) * pw]
        fc = pair_cols(f_col, c, ML_HEADS + h0, ML_HEADS + h1)
        ic = pair_cols(gc, c, h0, h1)
        w0 = (c // 2) * pw
        if c % 2 == 0:
            g_pair = jnp.where(left1, g_rows[h0:h0 + 1, w0:w0 + pw], g_up[h1:h1 + 1, w0:w0 + pw])
        else:
            g_pair = jnp.where(left1, g_dn[h0:h0 + 1, w0:w0 + pw], g_rows[h1:h1 + 1, w0:w0 + pw])
        log_d = jnp.where(causal, fc - g_pair, -jnp.inf)
        d_max = jnp.where(left,
                          jnp.max(jnp.where(left, log_d, -jnp.inf), axis=-1, keepdims=True),
                          jnp.max(jnp.where(left, -jnp.inf, log_d), axis=-1, keepdims=True))
        s_raw = _mm_nt(q, _bd(k, left))
        f_tot = fc[L - 1:L, :]
        lw = f_tot - fc + ic
        m_loc = jnp.max(lw, axis=0, keepdims=True)
        kw = k * jnp.exp(lw - m_loc)
        new = _mm_tn(kw, jnp.concatenate([v, ones_l], axis=1))
        top = jnp.concatenate([_bd(v, left), ones_bd], axis=1)
        pre.append((q, fc, log_d, d_max, s_raw, f_tot, m_loc, new, top))

    lhs, rhs, m_rows = [], [], []
    for p in range(npair):
        state = st_sc[p]
        m_in = m_sc[p:p + 1, :]
        for c in range(nc):
            q, fc, log_d, d_max, s_raw, f_tot, m_loc, new, top = pre[p * nc + c]
            m_inter = fc + m_in
            m_row = jnp.maximum(m_inter, d_max)
            sd = s_raw * jnp.exp(log_d - m_row)
            lhs.append(jnp.concatenate([sd, jnp.exp(m_inter - m_row) * q], axis=1))
            rhs.append(jnp.concatenate([top, state], axis=0))
            m_rows.append(m_row)
            m_new = jnp.maximum(f_tot + m_in, m_loc)
            a = jnp.exp(f_tot + m_in - m_new)
            b = jnp.exp(m_loc - m_new)
            a_col = jnp.where(top2, a[:, 0:1], a[:, HD:HD + 1])
            b_col = jnp.where(top2, b[:, 0:1], b[:, HD:HD + 1])
            state = a_col * state + jnp.where(diag2, b_col * new, 0.0)
            m_in = m_new
        st_sc[p] = state
        m_sc[p:p + 1, :] = m_in

    for i, (p, c) in enumerate(chains):
        rows = slice(c * L, (c + 1) * L)
        nd = _mm(lhs[i], rhs[i])
        hh = nd[:, 0:pw] / jnp.maximum(jnp.abs(nd[:, pw:]), jnp.exp(-m_rows[i]))
        ms = _mm(hh * hh, ones_bd) * (1.0 / HD)
        o = z_ref[rows, 3 * ML_DIM + p * pw:3 * ML_DIM + (p + 1) * pw]
        y_ref[rows, p * pw:(p + 1) * pw] = ((hh * lax.rsqrt(ms + EPS) * hn_ref[:, p * pw:(p + 1) * pw])
                                            * jax.nn.sigmoid(o))


def _mlstm(zml, gcol, grow, bcol, brow, head_norm, ts=256):
    b, s, _ = zml.shape
    ts = min(ts, s)
    nblk = s // ts
    const = lambda i, j: (0, 0)
    return pl.pallas_call(
        _mlstm_kernel,
        grid=(b, nblk),
        in_specs=[
            pl.BlockSpec((None, ts, 4 * ML_DIM), lambda i, j: (i, j, 0)),
            pl.BlockSpec((None, ts, GATE_PAD), lambda i, j: (i, j, 0)),
            pl.BlockSpec((2 * ML_HEADS, ts), lambda i, j: (0, i * nblk + j)),
            pl.BlockSpec((1, GATE_PAD), const),
            pl.BlockSpec((2 * ML_HEADS, 1), const),
            pl.BlockSpec((1, ML_DIM), const),
        ],
        out_specs=pl.BlockSpec((None, ts, ML_DIM), lambda i, j: (i, j, 0)),
        out_shape=jax.ShapeDtypeStruct((b, s, ML_DIM), F32),
        scratch_shapes=[
            pltpu.VMEM((ML_HEADS // 2, 2 * HD, 4 * HD), F32),
            pltpu.VMEM((8, 2 * HD), F32),
        ],
        compiler_params=_params("arbitrary", "arbitrary"),
    )(zml, gcol, grow, bcol, brow, head_norm)


def _shift_rows(x, d, fill, row):
    return jnp.where(row >= d, pltpu.roll(x, d, 0), fill)


def _lru_kernel(z_ref, cw_ref, cb_ref, wg_ref, bg_ref, lam_ref, y_ref, xbuf, h_sc):
    ts = z_ref.shape[0]

    @pl.when(pl.program_id(1) == 0)
    def _():
        xbuf[0:8, :] = jnp.zeros((8, LRU_DIM), F32)
        h_sc[...] = jnp.zeros_like(h_sc)

    xb = z_ref[:, 0:LRU_DIM]
    gb = z_ref[:, LRU_DIM:2 * LRU_DIM]
    xbuf[8:8 + ts, :] = xb
    xc = cb_ref[...] + cw_ref[CONV_W - 1:CONV_W, :] * xb
    for j in range(1, CONV_W):
        xc = xc + cw_ref[CONV_W - 1 - j:CONV_W - j, :] * xbuf[8 - j:8 - j + ts, :]
    xbuf[0:8, :] = xb[ts - 8:ts, :]

    gates = _mm(xc, wg_ref[...]) + bg_ref[...]
    r = jax.nn.sigmoid(gates[:, 0:LRU_DIM])
    i = jax.nn.sigmoid(gates[:, LRU_DIM:2 * LRU_DIM])
    log_a = -LRU_C * r * _softplus(-lam_ref[...])
    a = jnp.exp(log_a)
    u = jnp.sqrt(-jnp.tanh(log_a) * (a * a + 1.0)) * (i * xc)

    row = _iota2((ts, LRU_DIM), 0)
    acc_a, acc_h = a, u
    d = 1
    while d < ts:
        acc_h = acc_a * _shift_rows(acc_h, d, 0.0, row) + acc_h
        acc_a = acc_a * _shift_rows(acc_a, d, 1.0, row)
        d *= 2
    hfull = acc_a * h_sc[0:1, :] + acc_h
    h_sc[...] = jnp.broadcast_to(hfull[ts - 1:ts, :], h_sc.shape)

    gelu = 0.5 * gb * (1.0 + jnp.tanh(0.7978845608028654 * (gb + 0.044715 * (gb * gb * gb))))
    y_ref[...] = hfull * gelu


def _rglru(zlru, conv_w, conv_b, wg, bg, lam, ts=512):
    b, s, _ = zlru.shape
    ts = min(ts, s)
    const = lambda i, j: (0, 0)
    return pl.pallas_call(
        _lru_kernel,
        grid=(b, s // ts),
        in_specs=[
            pl.BlockSpec((None, ts, 2 * LRU_DIM), lambda i, j: (i, j, 0)),
            pl.BlockSpec((CONV_W, LRU_DIM), const),
            pl.BlockSpec((1, LRU_DIM), const),
            pl.BlockSpec((LRU_DIM, 2 * LRU_DIM), const),
            pl.BlockSpec((1, 2 * LRU_DIM), const),
            pl.BlockSpec((1, LRU_DIM), const),
        ],
        out_specs=pl.BlockSpec((None, ts, LRU_DIM), lambda i, j: (i, j, 0)),
        out_shape=jax.ShapeDtypeStruct((b, s, LRU_DIM), F32),
        scratch_shapes=[
            pltpu.VMEM((ts + 8, LRU_DIM), F32),
            pltpu.VMEM((8, LRU_DIM), F32),
        ],
        compiler_params=_params("arbitrary", "arbitrary"),
    )(zlru, conv_w, conv_b, wg, bg, lam)


def _rwkv_kernel(z_ref, mu_ref, wl_ref, w0_ref, a0_ref, kk_ref, ka_ref, rk_ref, lnw_ref, lnb_ref,
                 hsum_ref, y_ref, prev_sc, p_sc, o_sc):
    L = CHUNK
    ts = z_ref.shape[0]
    nc = ts // L
    npair = RW_HEADS // 2

    @pl.when(pl.program_id(1) == 0)
    def _():
        prev_sc[...] = jnp.zeros_like(prev_sc)
        p_sc[...] = jnp.zeros_like(p_sc)

    z = z_ref[...]
    row = _iota2(z.shape, 0)
    zprev = jnp.where(row == 0, prev_sc[0:1, :], pltpu.roll(z, 1, 0))
    prev_sc[...] = jnp.broadcast_to(z[ts - 1:ts, :], prev_sc.shape)
    z = z + (zprev - z) * mu_ref[...]

    r = z[:, 0:RW_DIM]
    k = z[:, RW_DIM:2 * RW_DIM]
    v = z[:, 2 * RW_DIM:3 * RW_DIM]
    lo = z[:, 3 * RW_DIM:3 * RW_DIM + RW_LORA_PAD]
    lane = _iota2(lo.shape, 1)
    lo = jnp.where(lane < 32, jnp.tanh(lo), jnp.where(lane < 64, lo, jax.nn.sigmoid(lo)))
    lora = _mm(lo, wl_ref[...])
    w = -_softplus(-(w0_ref[...] + lora[:, 0:RW_DIM])) - 0.5
    log_w = -jnp.exp(w)
    a = jax.nn.sigmoid(a0_ref[...] + lora[:, RW_DIM:2 * RW_DIM])
    g = lora[:, 2 * RW_DIM:3 * RW_DIM]

    hsum = hsum_ref[...]
    kk = k * kk_ref[...]
    kk = kk / jnp.maximum(jnp.sqrt(_mm(kk * kk, hsum)), 1e-12)
    k = k * (1.0 + (a - 1.0) * ka_ref[...])
    bonus = _mm(r * k * rk_ref[...], hsum) * v
    b = kk * a

    tri = (_iota2((L, L), 0) >= _iota2((L, L), 1)).astype(BF16)
    hi = log_w.astype(BF16)
    rest = log_w - hi.astype(F32)
    mid = rest.astype(BF16)
    lo3 = (rest - mid.astype(F32)).astype(BF16)
    pieces = jnp.concatenate([hi, mid, lo3], axis=1)
    cum = []
    for c in range(nc):
        s3 = jnp.dot(tri, pieces[c * L:(c + 1) * L, :], preferred_element_type=F32)
        cum.append(s3[:, 0:RW_DIM] + s3[:, RW_DIM:2 * RW_DIM] + s3[:, 2 * RW_DIM:])
    cum = jnp.concatenate(cum, axis=0)
    g_inv = jnp.exp(-cum)
    r_t = r * jnp.exp(cum)
    k_t = k * g_inv
    b_t = b * g_inv
    kp_t = kk * jnp.exp(cum - log_w)

    t_i = _iota2((L, 2 * HD), 0)
    lane = _iota2((L, 2 * HD), 1)
    s_i = lane & (HD - 1)
    left = lane < HD
    lower = t_i >= s_i
    strict = t_i > s_i
    eye = (t_i == s_i).astype(F32)
    levels = []
    bs = 1
    while bs < L:
        shift = bs.bit_length() - 1
        x, y = t_i >> shift, s_i >> shift
        levels.append(((x ^ y) == 1) & (x > y))
        bs *= 2

    chains = [(c, p) for c in range(nc) for p in range(npair)]
    cut = lambda arr, c, p: arr[c * L:(c + 1) * L, p * 2 * HD:(p + 1) * 2 * HD]

    rt, kpt, vv, qk, qb, ak, ab = [], [], [], [], [], [], []
    for c, p in chains:
        rt.append(cut(r_t, c, p))
        kpt.append(cut(kp_t, c, p))
        vv.append(cut(v, c, p))
        rhs = jnp.concatenate([_bd(cut(k_t, c, p), left), _bd(cut(b_t, c, p), left)], axis=0)
        sc = _mm_nt(jnp.concatenate([rt[-1], kpt[-1]], axis=0), rhs)
        qk.append(jnp.where(lower, sc[0:L, 0:2 * HD], 0.0))
        qb.append(jnp.where(lower, sc[0:L, 2 * HD:], 0.0))
        ak.append(jnp.where(strict, sc[L:, 0:2 * HD], 0.0))
        ab.append(jnp.where(strict, sc[L:, 2 * HD:], 0.0))

    tinv = [eye - jnp.where(levels[0], a, 0.0) for a in ab]
    for lvl in levels[1:]:
        xs = [_mm(jnp.where(lvl, a, 0.0), _bd(t, left)) for a, t in zip(ab, tinv)]
        tinv = [t - _mm(t, _bd(x, left)) for t, x in zip(tinv, xs)]

    qa = [_mm(jnp.concatenate([q, a], axis=0), _bd(v_, left)) for q, a, v_ in zip(qk, ak, vv)]
    ww = [_mm(t, jnp.concatenate([_bd(kp, left), _bd(q[L:], left)], axis=1))
          for t, kp, q in zip(tinv, kpt, qa)]
    qw = [_mm(q, jnp.concatenate([_bd(w[:, 0:2 * HD], left), _bd(w[:, 2 * HD:], left)], axis=1))
          for q, w in zip(qb, ww)]
    r_hat = [r_ - q[:, 0:2 * HD] for r_, q in zip(rt, qw)]
    o_loc = [q[0:L] - w[:, 2 * HD:] for q, w in zip(qa, qw)]

    m_st, n_st = [], []
    for i, (c, p) in enumerate(chains):
        cum_c = cut(cum, c, p)
        cum_end = cum_c[L - 1:L, :]
        g_end = jnp.exp(cum_end - cum_c)
        g1 = _mm_tn(cut(b, c, p) * g_end, ww[i])
        g2 = _mm_tn(cut(k, c, p) * g_end, vv[i])
        m_st.append(eye * jnp.exp(cum_end) - _unbd(g1[:, 0:2 * HD], left))
        n_st.append(_unbd(g2, left) - _unbd(g1[:, 2 * HD:], left))

    state = [p_sc[p] for p in range(npair)]
    for i, (c, p) in enumerate(chains):
        op = _mm(jnp.concatenate([r_hat[i], m_st[i]], axis=0), _bd(state[p], left))
        o_sc[c * L:(c + 1) * L, p * 2 * HD:(p + 1) * 2 * HD] = op[0:L] + o_loc[i]
        state[p] = op[L:] + n_st[i]
    for p in range(npair):
        p_sc[p] = state[p]
    o = o_sc[...]

    mean = _mm(o, hsum) * (1.0 / HD)
    dev = o - mean
    var = _mm(dev * dev, hsum) * (1.0 / HD)
    o = dev * lax.rsqrt(var + RW_GN_EPS) * lnw_ref[...] + lnb_ref[...]
    y_ref[...] = (o + bonus) * g


def _rwkv(zrw, mu, wl, w0, a0, k_k, k_a, r_k, ln_w, ln_b, hsum, ts=256):
    b, s, _ = zrw.shape
    ts = min(ts, s)
    const = lambda i, j: (0, 0)
    vec = pl.BlockSpec((1, RW_DIM), const)
    return pl.pallas_call(
        _rwkv_kernel,
        grid=(b, s // ts),
        in_specs=[
            pl.BlockSpec((None, ts, RW_IN_PAD), lambda i, j: (i, j, 0)),
            pl.BlockSpec((1, RW_IN_PAD), const),
            pl.BlockSpec((RW_LORA_PAD, 3 * RW_DIM), const),
            vec, vec, vec, vec, vec, vec, vec,
            pl.BlockSpec((RW_DIM, RW_DIM), const),
        ],
        out_specs=pl.BlockSpec((None, ts, RW_DIM), lambda i, j: (i, j, 0)),
        out_shape=jax.ShapeDtypeStruct((b, s, RW_DIM), F32),
        scratch_shapes=[
            pltpu.VMEM((8, RW_IN_PAD), F32),
            pltpu.VMEM((RW_HEADS // 2, HD, 2 * HD), F32),
            pltpu.VMEM((ts, RW_DIM), F32),
        ],
        compiler_params=_params("arbitrary", "arbitrary"),
    )(zrw, mu, wl, w0, a0, k_k, k_a, r_k, ln_w, ln_b, hsum)


def _tail_kernel(yml_ref, ylru_ref, yrw_ref, x_ref, p_ref, wo_ref, gmix_ref, gpre_ref, w1_ref, w2_ref,
                 gffn_ref, wg_ref, wp_ref, gple_ref, o_ref, x1_sc, h_sc, acc_sc):
    j = pl.program_id(1)

    @pl.when(j == 0)
    def _():
        mix = (_mm(yml_ref[...], wo_ref[0:ML_DIM, :])
               + _mm(ylru_ref[...], wo_ref[ML_DIM:ML_DIM + LRU_DIM, :])
               + _mm(yrw_ref[...], wo_ref[ML_DIM + LRU_DIM:, :]))
        x1 = x_ref[...] + _rms(mix, gmix_ref[...])
        x1_sc[...] = x1
        h_sc[...] = _rms(x1, gpre_ref[...]).astype(BF16)
        acc_sc[...] = jnp.zeros_like(acc_sc)

    u = jnp.maximum(jnp.dot(h_sc[...], w1_ref[...], preferred_element_type=F32), 0.0)
    acc_sc[...] += _mm(u * u, w2_ref[...])

    @pl.when(j == pl.num_programs(1) - 1)
    def _():
        x2 = x1_sc[...] + _rms(acc_sc[...], gffn_ref[...])
        e = jax.nn.sigmoid(_mm(x2, wg_ref[...])) * _mm(p_ref[...], wp_ref[...])
        o_ref[...] = x2 + _rms(e, gple_ref[...])


def _tail(yml, ylru, yrw, x2, p2, wo, gmix, gpre, w1, w2, gffn, wg, wp, gple, tm=512, tf=1024):
    t, d = x2.shape
    ff = w1.shape[1]
    dp = p2.shape[1]
    const = lambda i, j: (0, 0)
    row = lambda i, j: (i, 0)
    vec = pl.BlockSpec((1, d), const)
    return pl.pallas_call(
        _tail_kernel,
        grid=(t // tm, ff // tf),
        in_specs=[
            pl.BlockSpec((tm, ML_DIM), row),
            pl.BlockSpec((tm, LRU_DIM), row),
            pl.BlockSpec((tm, RW_DIM), row),
            pl.BlockSpec((tm, d), row),
            pl.BlockSpec((tm, dp), row),
            pl.BlockSpec(wo.shape, const),
            vec,
            vec,
            pl.BlockSpec((d, tf), lambda i, j: (0, j)),
            pl.BlockSpec((tf, d), lambda i, j: (j, 0)),
            vec,
            pl.BlockSpec((d, d), const),
            pl.BlockSpec((dp, d), const),
            vec,
        ],
        out_specs=pl.BlockSpec((tm, d), row),
        out_shape=jax.ShapeDtypeStruct((t, d), F32),
        scratch_shapes=[pltpu.VMEM((tm, d), F32), pltpu.VMEM((tm, d), BF16), pltpu.VMEM((tm, d), F32)],
        compiler_params=_params("arbitrary", "arbitrary"),
    )(yml, ylru, yrw, x2, p2, wo, gmix, gpre, w1, w2, gffn, wg, wp, gple)


def _pad_cols(w, n):
    return jnp.pad(w, ((0, 0), (0, n - w.shape[1])))


def _block_diag(blocks):
    n, d, e = blocks.shape
    eye = jnp.eye(n, dtype=blocks.dtype)
    return (eye[:, None, :, None] * blocks[:, :, None, :]).reshape(n * d, n * e)


def _mixers(z_parts, b, s, ml_gate_bias, ml_head_norm, lru, rw):
    zml, zlru, zrw, gcol, grow = z_parts
    zml = zml.reshape(b, s, -1)
    zlru = zlru.reshape(b, s, -1)
    zrw = zrw.reshape(b, s, -1)
    gcol = gcol.reshape(b, s, -1)
    bias = ml_gate_bias.reshape(1, 2 * ML_HEADS)
    yml = _mlstm(zml, gcol, grow, _pad_cols(bias, GATE_PAD), bias.reshape(-1, 1),
                 ml_head_norm.reshape(1, -1))
    ylru = _rglru(zlru, *lru)
    yrw = _rwkv(zrw, *rw)
    return yml, ylru, yrw


def kernel(x, p, norm_mix_pre, norm_mix_post, norm_ffn_pre, norm_ffn_post, norm_ple, w_in, w_out, ml_gate_bias, ml_head_norm, lru_conv_w, lru_conv_b, lru_w_r, lru_b_r, lru_w_i, lru_b_i, lru_lambda, rw_mu, rw_w0, rw_w2, rw_a0, rw_a2, rw_g2, rw_k_k, rw_k_a, rw_r_k, rw_ln_w, rw_ln_b, ffn_w1, ffn_w2, ple_w_proj, ple_w_gate):
    b, s, d = x.shape
    depth = w_in.shape[0]
    t = b * s
    ml_in = 4 * ML_DIM + 2 * ML_HEADS
    lru_in = 2 * LRU_DIM
    head_id = jnp.arange(RW_DIM) // HD
    hsum = (head_id[:, None] == head_id[None, :]).astype(BF16)
    row = lambda v: v.reshape(1, -1)

    x2 = x.reshape(t, d)
    for l in range(depth):
        wi = w_in[l]
        w_gate = wi[:, 4 * ML_DIM:ml_in]
        z_parts = _in_proj(
            x2, row(norm_mix_pre[l]),
            wi[:, :4 * ML_DIM].astype(BF16),
            wi[:, ml_in:ml_in + lru_in].astype(BF16),
            _pad_cols(wi[:, ml_in + lru_in:], RW_IN_PAD).astype(BF16),
            _pad_cols(w_gate, GATE_PAD).astype(BF16),
            w_gate.T.astype(BF16))

        lru = (lru_conv_w[l], row(lru_conv_b[l]),
               jnp.concatenate([_block_diag(lru_w_r[l]), _block_diag(lru_w_i[l])], axis=1).astype(BF16),
               jnp.concatenate([row(lru_b_r[l]), row(lru_b_i[l])], axis=1),
               row(lru_lambda[l]))

        w_lora = jnp.zeros((RW_LORA_PAD, 3 * RW_DIM), F32)
        w_lora = w_lora.at[0:32, 0:RW_DIM].set(rw_w2[l])
        w_lora = w_lora.at[32:64, RW_DIM:2 * RW_DIM].set(rw_a2[l])
        w_lora = w_lora.at[64:RW_LORA, 2 * RW_DIM:].set(rw_g2[l])
        rw = (_pad_cols(row(rw_mu[l]), RW_IN_PAD), w_lora.astype(BF16), row(rw_w0[l]), row(rw_a0[l]),
              row(rw_k_k[l]), row(rw_k_a[l]), row(rw_r_k[l]), row(rw_ln_w[l]), row(rw_ln_b[l]), hsum)

        yml, ylru, yrw = _mixers(z_parts, b, s, ml_gate_bias[l], ml_head_norm[l], lru, rw)
        x2 = _tail(yml.reshape(t, -1), ylru.reshape(t, -1), yrw.reshape(t, -1), x2, p[l].reshape(t, -1),
                   w_out[l].astype(BF16), row(norm_mix_post[l]), row(norm_ffn_pre[l]),
                   ffn_w1[l].astype(BF16), ffn_w2[l].astype(BF16), row(norm_ffn_post[l]),
                   ple_w_gate[l].astype(BF16), ple_w_proj[l].astype(BF16), row(norm_ple[l]))
    return x2.reshape(b, s, d)
```

```python
import functools

import jax
import jax.numpy as jnp
from jax import lax
from jax.experimental import pallas as pl
from jax.experimental.pallas import tpu as pltpu

F32 = jnp.float32
BF16 = jnp.bfloat16

EPS = 1e-6
HD = 64
ML_HEADS = 4
ML_DIM = ML_HEADS * HD
LRU_DIM = 256
LRU_C = 8.0
CONV_W = 4
RW_HEADS = 8
RW_DIM = RW_HEADS * HD
RW_LORA = 160
RW_LORA_PAD = 256
RW_IN_PAD = 3 * RW_DIM + RW_LORA_PAD
RW_GN_EPS = 64e-5
CHUNK = 64
GATE_PAD = 128

VMEM_LIMIT = 48 * 1024 * 1024

_NT = (((1,), (1,)), ((), ()))
_TN = (((0,), (0,)), ((), ()))


def _mm(a, b):
    return jnp.dot(a.astype(BF16), b.astype(BF16), preferred_element_type=F32)


def _mm_nt(a, b):
    return lax.dot_general(a.astype(BF16), b.astype(BF16), _NT, preferred_element_type=F32)


def _mm_tn(a, b):
    return lax.dot_general(a.astype(BF16), b.astype(BF16), _TN, preferred_element_type=F32)


def _mm_f32(a, b):
    return jnp.dot(a, b, preferred_element_type=F32, precision=lax.Precision.HIGHEST)


def _rms(x, gain):
    return x * lax.rsqrt(jnp.mean(x * x, axis=-1, keepdims=True) + EPS) * gain


def _softplus(x):
    return jnp.maximum(x, 0.0) + jnp.log(1.0 + jnp.exp(-jnp.abs(x)))


def _log_sigmoid(x):
    return jnp.minimum(x, 0.0) - jnp.log(1.0 + jnp.exp(-jnp.abs(x)))


def _iota2(shape, dim):
    return lax.broadcasted_iota(jnp.int32, shape, dim)


def _params(*sem):
    return pltpu.CompilerParams(dimension_semantics=sem, vmem_limit_bytes=VMEM_LIMIT)


def _bd(y, left):
    return jnp.concatenate([jnp.where(left, y, 0.0), jnp.where(left, 0.0, y)], axis=0)


def _unbd(z, left):
    return jnp.where(left, z[0:HD, :], z[HD:2 * HD, :])


def _in_proj_kernel(x_ref, g_ref, wml_ref, wlru_ref, wrw_ref, wg_ref, wgt_ref,
                    zml_ref, zlru_ref, zrw_ref, gcol_ref, grow_ref):
    hb = _rms(x_ref[...], g_ref[...]).astype(BF16)
    zml_ref[...] = jnp.dot(hb, wml_ref[...], preferred_element_type=F32)
    zlru_ref[...] = jnp.dot(hb, wlru_ref[...], preferred_element_type=F32)
    zrw_ref[...] = jnp.dot(hb, wrw_ref[...], preferred_element_type=F32)
    gcol_ref[...] = jnp.dot(hb, wg_ref[...], preferred_element_type=F32)
    grow_ref[...] = lax.dot_general(wgt_ref[...], hb, _NT, preferred_element_type=F32)


def _in_proj(x2, gain, wml, wlru, wrw, wg, wgt, tm=512):
    t, d = x2.shape
    const = lambda i: (0, 0)
    row = lambda i: (i, 0)
    outs = (
        jax.ShapeDtypeStruct((t, wml.shape[1]), F32),
        jax.ShapeDtypeStruct((t, wlru.shape[1]), F32),
        jax.ShapeDtypeStruct((t, wrw.shape[1]), F32),
        jax.ShapeDtypeStruct((t, GATE_PAD), F32),
        jax.ShapeDtypeStruct((wgt.shape[0], t), F32),
    )
    return pl.pallas_call(
        _in_proj_kernel,
        grid=(t // tm,),
        in_specs=[
            pl.BlockSpec((tm, d), row),
            pl.BlockSpec((1, d), const),
            pl.BlockSpec(wml.shape, const),
            pl.BlockSpec(wlru.shape, const),
            pl.BlockSpec(wrw.shape, const),
            pl.BlockSpec(wg.shape, const),
            pl.BlockSpec(wgt.shape, const),
        ],
        out_specs=(
            pl.BlockSpec((tm, wml.shape[1]), row),
            pl.BlockSpec((tm, wlru.shape[1]), row),
            pl.BlockSpec((tm, wrw.shape[1]), row),
            pl.BlockSpec((tm, GATE_PAD), row),
            pl.BlockSpec((wgt.shape[0], tm), lambda i: (0, i)),
        ),
        out_shape=outs,
        compiler_params=_params("arbitrary"),
    )(x2, gain, wml, wlru, wrw, wg, wgt)


def _mlstm_kernel(z_ref, gcol_ref, grow_ref, bcol_ref, brow_ref, hn_ref, y_ref, st_sc, m_sc):
    L = CHUNK
    ts = z_ref.shape[0]
    nc = ts // L
    npair = ML_HEADS // 2
    pw = 2 * HD

    @pl.when(pl.program_id(1) == 0)
    def _():
        st_sc[...] = jnp.zeros_like(st_sc)
        m_sc[...] = jnp.zeros_like(m_sc)

    rt_i = _iota2((ts, ts), 0)
    ct_i = _iota2((ts, ts), 1)
    same = (rt_i // L) == (ct_i // L)
    gc = gcol_ref[...] + bcol_ref[...]
    f_col = _mm_f32(((rt_i >= ct_i) & same).astype(F32), _log_sigmoid(gc))
    gr = grow_ref[...] + brow_ref[...]
    f_row = _mm_f32(_log_sigmoid(gr), ((rt_i <= ct_i) & same).astype(F32))
    g_rows = f_row[ML_HEADS:2 * ML_HEADS, :] - gr[0:ML_HEADS, :]
    g_up = pltpu.roll(g_rows, HD, 1)
    g_dn = pltpu.roll(g_rows, ts - HD, 1)

    t_i = _iota2((L, pw), 0)
    lane = _iota2((L, pw), 1)
    left = lane < HD
    left1 = left[0:1, :]
    causal = t_i >= (lane & (HD - 1))
    r2 = _iota2((pw, 2 * pw), 0)
    c2 = _iota2((pw, 2 * pw), 1)
    diag2 = (r2 // HD) == ((c2 // HD) % 2)
    top2 = r2 < HD
    ones_bd = diag2[:, 0:pw].astype(F32)
    ones_l = jnp.ones((L, pw), F32)
    scale = HD ** -0.5

    def pair_cols(arr, c, j0, j1):
        rows = arr[c * L:(c + 1) * L, :]
        return jnp.where(left, rows[:, j0:j0 + 1], rows[:, j1:j1 + 1])

    chains = [(p, c) for p in range(npair) for c in range(nc)]

    pre = []
    for p, c in chains:
        h0, h1 = 2 * p, 2 * p + 1
        rows = slice(c * L, (c + 1) * L)
        q = z_ref[rows, p * pw:(p + 1) * pw] * scale
        k = z_ref[rows, ML_DIM + p * pw:ML_DIM + (p + 1) * pw]
        v = z_ref[rows, 2 * ML_DIM + p * pw:2 * ML_DIM + (p + 1) * pw]
        fc = pair_cols(f_col, c, ML_HEADS + h0, ML_HEADS + h1)
        ic = pair_cols(gc, c, h0, h1)
        w0 = (c // 2) * pw
        if c % 2 == 0:
            g_pair = jnp.where(left1, g_rows[h0:h0 + 1, w0:w0 + pw], g_up[h1:h1 + 1, w0:w0 + pw])
        else:
            g_pair = jnp.where(left1, g_dn[h0:h0 + 1, w0:w0 + pw], g_rows[h1:h1 + 1, w0:w0 + pw])
        log_d = jnp.where(causal, fc - g_pair, -jnp.inf)
        d_max = jnp.where(left,
                          jnp.max(jnp.where(left, log_d, -jnp.inf), axis=-1, keepdims=True),
                          jnp.max(jnp.where(left, -jnp.inf, log_d), axis=-1, keepdims=True))
        s_raw = _mm_nt(q, _bd(k, left))
        f_tot = fc[L - 1:L, :]
        lw = f_tot - fc + ic
        m_loc = jnp.max(lw, axis=0, keepdims=True)
        kw = k * jnp.exp(lw - m_loc)
        new = _mm_tn(kw, jnp.concatenate([v, ones_l], axis=1))
        top = jnp.concatenate([_bd(v, left), ones_bd], axis=1)
        pre.append((q, fc, log_d, d_max, s_raw, f_tot, m_loc, new, top))

    lhs, rhs, m_rows = [], [], []
    for p in range(npair):
        state = st_sc[p]
        m_in = m_sc[p:p + 1, :]
        for c in range(nc):
            q, fc, log_d, d_max, s_raw, f_tot, m_loc, new, top = pre[p * nc + c]
            m_inter = fc + m_in
            m_row = jnp.maximum(m_inter, d_max)
            sd = s_raw * jnp.exp(log_d - m_row)
            lhs.append(jnp.concatenate([sd, jnp.exp(m_inter - m_row) * q], axis=1))
            rhs.append(jnp.concatenate([top, state], axis=0))
            m_rows.append(m_row)
            m_new = jnp.maximum(f_tot + m_in, m_loc)
            a = jnp.exp(f_tot + m_in - m_new)
            b = jnp.exp(m_loc - m_new)
            a_col = jnp.where(top2, a[:, 0:1], a[:, HD:HD + 1])
            b_col = jnp.where(top2, b[:, 0:1], b[:, HD:HD + 1])
            state = a_col * state + jnp.where(diag2, b_col * new, 0.0)
            m_in = m_new
        st_sc[p] = state
        m_sc[p:p + 1, :] = m_in

    for i, (p, c) in enumerate(chains):
        rows = slice(c * L, (c + 1) * L)
        nd = _mm(lhs[i], rhs[i])
        hh = nd[:, 0:pw] / jnp.maximum(jnp.abs(nd[:, pw:]), jnp.exp(-m_rows[i]))
        ms = _mm(hh * hh, ones_bd) * (1.0 / HD)
        o = z_ref[rows, 3 * ML_DIM + p * pw:3 * ML_DIM + (p + 1) * pw]
        y_ref[rows, p * pw:(p + 1) * pw] = ((hh * lax.rsqrt(ms + EPS) * hn_ref[:, p * pw:(p + 1) * pw])
                                            * jax.nn.sigmoid(o))


def _mlstm(zml, gcol, grow, bcol, brow, head_norm, ts=256):
    b, s, _ = zml.shape
    ts = min(ts, s)
    nblk = s // ts
    const = lambda i, j: (0, 0)
    return pl.pallas_call(
        _mlstm_kernel,
        grid=(b, nblk),
        in_specs=[
            pl.BlockSpec((None, ts, 4 * ML_DIM), lambda i, j: (i, j, 0)),
            pl.BlockSpec((None, ts, GATE_PAD), lambda i, j: (i, j, 0)),
            pl.BlockSpec((2 * ML_HEADS, ts), lambda i, j: (0, i * nblk + j)),
            pl.BlockSpec((1, GATE_PAD), const),
            pl.BlockSpec((2 * ML_HEADS, 1), const),
            pl.BlockSpec((1, ML_DIM), const),
        ],
        out_specs=pl.BlockSpec((None, ts, ML_DIM), lambda i, j: (i, j, 0)),
        out_shape=jax.ShapeDtypeStruct((b, s, ML_DIM), F32),
        scratch_shapes=[
            pltpu.VMEM((ML_HEADS // 2, 2 * HD, 4 * HD), F32),
            pltpu.VMEM((8, 2 * HD), F32),
        ],
        compiler_params=_params("arbitrary", "arbitrary"),
    )(zml, gcol, grow, bcol, brow, head_norm)


def _shift_rows(x, d, fill, row):
    return jnp.where(row >= d, pltpu.roll(x, d, 0), fill)


def _lru_kernel(z_ref, cw_ref, cb_ref, wg_ref, bg_ref, lam_ref, y_ref, xbuf, h_sc):
    ts = z_ref.shape[0]

    @pl.when(pl.program_id(1) == 0)
    def _():
        xbuf[0:8, :] = jnp.zeros((8, LRU_DIM), F32)
        h_sc[...] = jnp.zeros_like(h_sc)

    xb = z_ref[:, 0:LRU_DIM]
    gb = z_ref[:, LRU_DIM:2 * LRU_DIM]
    xbuf[8:8 + ts, :] = xb
    xc = cb_ref[...] + cw_ref[CONV_W - 1:CONV_W, :] * xb
    for j in range(1, CONV_W):
        xc = xc + cw_ref[CONV_W - 1 - j:CONV_W - j, :] * xbuf[8 - j:8 - j + ts, :]
    xbuf[0:8, :] = xb[ts - 8:ts, :]

    gates = _mm(xc, wg_ref[...]) + bg_ref[...]
    r = jax.nn.sigmoid(gates[:, 0:LRU_DIM])
    i = jax.nn.sigmoid(gates[:, LRU_DIM:2 * LRU_DIM])
    log_a = -LRU_C * r * _softplus(-lam_ref[...])
    a = jnp.exp(log_a)
    u = jnp.sqrt(-jnp.tanh(log_a) * (a * a + 1.0)) * (i * xc)

    row = _iota2((ts, LRU_DIM), 0)
    acc_a, acc_h = a, u
    d = 1
    while d < ts:
        acc_h = acc_a * _shift_rows(acc_h, d, 0.0, row) + acc_h
        acc_a = acc_a * _shift_rows(acc_a, d, 1.0, row)
        d *= 2
    hfull = acc_a * h_sc[0:1, :] + acc_h
    h_sc[...] = jnp.broadcast_to(hfull[ts - 1:ts, :], h_sc.shape)

    gelu = 0.5 * gb * (1.0 + jnp.tanh(0.7978845608028654 * (gb + 0.044715 * (gb * gb * gb))))
    y_ref[...] = hfull * gelu


def _rglru(zlru, conv_w, conv_b, wg, bg, lam, ts=512):
    b, s, _ = zlru.shape
    ts = min(ts, s)
    const = lambda i, j: (0, 0)
    return pl.pallas_call(
        _lru_kernel,
        grid=(b, s // ts),
        in_specs=[
            pl.BlockSpec((None, ts, 2 * LRU_DIM), lambda i, j: (i, j, 0)),
            pl.BlockSpec((CONV_W, LRU_DIM), const),
            pl.BlockSpec((1, LRU_DIM), const),
            pl.BlockSpec((LRU_DIM, 2 * LRU_DIM), const),
            pl.BlockSpec((1, 2 * LRU_DIM), const),
            pl.BlockSpec((1, LRU_DIM), const),
        ],
        out_specs=pl.BlockSpec((None, ts, LRU_DIM), lambda i, j: (i, j, 0)),
        out_shape=jax.ShapeDtypeStruct((b, s, LRU_DIM), F32),
        scratch_shapes=[
            pltpu.VMEM((ts + 8, LRU_DIM), F32),
            pltpu.VMEM((8, LRU_DIM), F32),
        ],
        compiler_params=_params("arbitrary", "arbitrary"),
    )(zlru, conv_w, conv_b, wg, bg, lam)


def _rwkv_kernel(z_ref, mu_ref, wl_ref, w0_ref, a0_ref, kk_ref, ka_ref, rk_ref, lnw_ref, lnb_ref,
                 hsum_ref, y_ref, prev_sc, p_sc, o_sc):
    L = CHUNK
    ts = z_ref.shape[0]
    nc = ts // L
    npair = RW_HEADS // 2

    @pl.when(pl.program_id(1) == 0)
    def _():
        prev_sc[...] = jnp.zeros_like(prev_sc)
        p_sc[...] = jnp.zeros_like(p_sc)

    z = z_ref[...]
    row = _iota2(z.shape, 0)
    zprev = jnp.where(row == 0, prev_sc[0:1, :], pltpu.roll(z, 1, 0))
    prev_sc[...] = jnp.broadcast_to(z[ts - 1:ts, :], prev_sc.shape)
    z = z + (zprev - z) * mu_ref[...]

    r = z[:, 0:RW_DIM]
    k = z[:, RW_DIM:2 * RW_DIM]
    v = z[:, 2 * RW_DIM:3 * RW_DIM]
    lo = z[:, 3 * RW_DIM:3 * RW_DIM + RW_LORA_PAD]
    lane = _iota2(lo.shape, 1)
    lo = jnp.where(lane < 32, jnp.tanh(lo), jnp.where(lane < 64, lo, jax.nn.sigmoid(lo)))
    lora = _mm(lo, wl_ref[...])
    w = -_softplus(-(w0_ref[...] + lora[:, 0:RW_DIM])) - 0.5
    log_w = -jnp.exp(w)
    a = jax.nn.sigmoid(a0_ref[...] + lora[:, RW_DIM:2 * RW_DIM])
    g = lora[:, 2 * RW_DIM:3 * RW_DIM]

    hsum = hsum_ref[...]
    kk = k * kk_ref[...]
    kk = kk / jnp.maximum(jnp.sqrt(_mm(kk * kk, hsum)), 1e-12)
    k = k * (1.0 + (a - 1.0) * ka_ref[...])
    bonus = _mm(r * k * rk_ref[...], hsum) * v
    b = kk * a

    tri = (_iota2((L, L), 0) >= _iota2((L, L), 1)).astype(BF16)
    hi = log_w.astype(BF16)
    rest = log_w - hi.astype(F32)
    mid = rest.astype(BF16)
    lo3 = (rest - mid.astype(F32)).astype(BF16)
    pieces = jnp.concatenate([hi, mid, lo3], axis=1)
    cum = []
    for c in range(nc):
        s3 = jnp.dot(tri, pieces[c * L:(c + 1) * L, :], preferred_element_type=F32)
        cum.append(s3[:, 0:RW_DIM] + s3[:, RW_DIM:2 * RW_DIM] + s3[:, 2 * RW_DIM:])
    cum = jnp.concatenate(cum, axis=0)
    g_inv = jnp.exp(-cum)
    r_t = r * jnp.exp(cum)
    k_t = k * g_inv
    b_t = b * g_inv
    kp_t = kk * jnp.exp(cum - log_w)

    t_i = _iota2((L, 2 * HD), 0)
    lane = _iota2((L, 2 * HD), 1)
    s_i = lane & (HD - 1)
    left = lane < HD
    lower = t_i >= s_i
    strict = t_i > s_i
    eye = (t_i == s_i).astype(F32)
    levels = []
    bs = 1
    while bs < L:
        shift = bs.bit_length() - 1
        x, y = t_i >> shift, s_i >> shift
        levels.append(((x ^ y) == 1) & (x > y))
        bs *= 2

    chains = [(c, p) for c in range(nc) for p in range(npair)]
    cut = lambda arr, c, p: arr[c * L:(c + 1) * L, p * 2 * HD:(p + 1) * 2 * HD]

    rt, kpt, vv, qk, qb, ak, ab = [], [], [], [], [], [], []
    for c, p in chains:
        rt.append(cut(r_t, c, p))
        kpt.append(cut(kp_t, c, p))
        vv.append(cut(v, c, p))
        rhs = jnp.concatenate([_bd(cut(k_t, c, p), left), _bd(cut(b_t, c, p), left)], axis=0)
        sc = _mm_nt(jnp.concatenate([rt[-1], kpt[-1]], axis=0), rhs)
        qk.append(jnp.where(lower, sc[0:L, 0:2 * HD], 0.0))
        qb.append(jnp.where(lower, sc[0:L, 2 * HD:], 0.0))
        ak.append(jnp.where(strict, sc[L:, 0:2 * HD], 0.0))
        ab.append(jnp.where(strict, sc[L:, 2 * HD:], 0.0))

    tinv = [eye - jnp.where(levels[0], a, 0.0) for a in ab]
    for lvl in levels[1:]:
        xs = [_mm(jnp.where(lvl, a, 0.0), _bd(t, left)) for a, t in zip(ab, tinv)]
        tinv = [t - _mm(t, _bd(x, left)) for t, x in zip(tinv, xs)]

    qa = [_mm(jnp.concatenate([q, a], axis=0), _bd(v_, left)) for q, a, v_ in zip(qk, ak, vv)]
    ww = [_mm(t, jnp.concatenate([_bd(kp, left), _bd(q[L:], left)], axis=1))
          for t, kp, q in zip(tinv, kpt, qa)]
    qw = [_mm(q, jnp.concatenate([_bd(w[:, 0:2 * HD], left), _bd(w[:, 2 * HD:], left)], axis=1))
          for q, w in zip(qb, ww)]
    r_hat = [r_ - q[:, 0:2 * HD] for r_, q in zip(rt, qw)]
    o_loc = [q[0:L] - w[:, 2 * HD:] for q, w in zip(qa, qw)]

    m_st, n_st = [], []
    for i, (c, p) in enumerate(chains):
        cum_c = cut(cum, c, p)
        cum_end = cum_c[L - 1:L, :]
        g_end = jnp.exp(cum_end - cum_c)
        g1 = _mm_tn(cut(b, c, p) * g_end, ww[i])
        g2 = _mm_tn(cut(k, c, p) * g_end, vv[i])
        m_st.append(eye * jnp.exp(cum_end) - _unbd(g1[:, 0:2 * HD], left))
        n_st.append(_unbd(g2, left) - _unbd(g1[:, 2 * HD:], left))

    state = [p_sc[p] for p in range(npair)]
    for i, (c, p) in enumerate(chains):
        op = _mm(jnp.concatenate([r_hat[i], m_st[i]], axis=0), _bd(state[p], left))
        o_sc[c * L:(c + 1) * L, p * 2 * HD:(p + 1) * 2 * HD] = op[0:L] + o_loc[i]
        state[p] = op[L:] + n_st[i]
    for p in range(npair):
        p_sc[p] = state[p]
    o = o_sc[...]

    mean = _mm(o, hsum) * (1.0 / HD)
    dev = o - mean
    var = _mm(dev * dev, hsum) * (1.0 / HD)
    o = dev * lax.rsqrt(var + RW_GN_EPS) * lnw_ref[...] + lnb_ref[...]
    y_ref[...] = (o + bonus) * g


def _rwkv(zrw, mu, wl, w0, a0, k_k, k_a, r_k, ln_w, ln_b, hsum, ts=256):
    b, s, _ = zrw.shape
    ts = min(ts, s)
    const = lambda i, j: (0, 0)
    vec = pl.BlockSpec((1, RW_DIM), const)
    return pl.pallas_call(
        _rwkv_kernel,
        grid=(b, s // ts),
        in_specs=[
            pl.BlockSpec((None, ts, RW_IN_PAD), lambda i, j: (i, j, 0)),
            pl.BlockSpec((1, RW_IN_PAD), const),
            pl.BlockSpec((RW_LORA_PAD, 3 * RW_DIM), const),
            vec, vec, vec, vec, vec, vec, vec,
            pl.BlockSpec((RW_DIM, RW_DIM), const),
        ],
        out_specs=pl.BlockSpec((None, ts, RW_DIM), lambda i, j: (i, j, 0)),
        out_shape=jax.ShapeDtypeStruct((b, s, RW_DIM), F32),
        scratch_shapes=[
            pltpu.VMEM((8, RW_IN_PAD), F32),
            pltpu.VMEM((RW_HEADS // 2, HD, 2 * HD), F32),
            pltpu.VMEM((ts, RW_DIM), F32),
        ],
        compiler_params=_params("arbitrary", "arbitrary"),
    )(zrw, mu, wl, w0, a0, k_k, k_a, r_k, ln_w, ln_b, hsum)


def _tail_kernel(yml_ref, ylru_ref, yrw_ref, x_ref, p_ref, wo_ref, gmix_ref, gpre_ref, w1_ref, w2_ref,
                 gffn_ref, wg_ref, wp_ref, gple_ref, o_ref, *, tf):
    mix = (_mm(yml_ref[...], wo_ref[0:ML_DIM, :])
           + _mm(ylru_ref[...], wo_ref[ML_DIM:ML_DIM + LRU_DIM, :])
           + _mm(yrw_ref[...], wo_ref[ML_DIM + LRU_DIM:, :]))
    x1 = x_ref[...] + _rms(mix, gmix_ref[...])
    h = _rms(x1, gpre_ref[...]).astype(BF16)
    acc = None
    for c in range(w1_ref.shape[1] // tf):
        u = jnp.maximum(jnp.dot(h, w1_ref[:, c * tf:(c + 1) * tf], preferred_element_type=F32), 0.0)
        part = _mm(u * u, w2_ref[c * tf:(c + 1) * tf, :])
        acc = part if acc is None else acc + part
    x2 = x1 + _rms(acc, gffn_ref[...])
    e = jax.nn.sigmoid(_mm(x2, wg_ref[...])) * _mm(p_ref[...], wp_ref[...])
    o_ref[...] = x2 + _rms(e, gple_ref[...])


def _tail(yml, ylru, yrw, x2, p2, wo, gmix, gpre, w1, w2, gffn, wg, wp, gple, tm=512, tf=1024):
    t, d = x2.shape
    dp = p2.shape[1]
    const = lambda i: (0, 0)
    row = lambda i: (i, 0)
    resident = lambda w: pl.BlockSpec(w.shape, const, pipeline_mode=pl.Buffered(1))
    vec = pl.BlockSpec((1, d), const)
    return pl.pallas_call(
        functools.partial(_tail_kernel, tf=tf),
        grid=(t // tm,),
        in_specs=[
            pl.BlockSpec((tm, ML_DIM), row),
            pl.BlockSpec((tm, LRU_DIM), row),
            pl.BlockSpec((tm, RW_DIM), row),
            pl.BlockSpec((tm, d), row),
            pl.BlockSpec((tm, dp), row),
            resident(wo), vec, vec, resident(w1), resident(w2), vec, resident(wg), resident(wp), vec,
        ],
        out_specs=pl.BlockSpec((tm, d), row),
        out_shape=jax.ShapeDtypeStruct((t, d), F32),
        compiler_params=_params("arbitrary"),
    )(yml, ylru, yrw, x2, p2, wo, gmix, gpre, w1, w2, gffn, wg, wp, gple)


def _pad_cols(w, n):
    return jnp.pad(w, ((0, 0), (0, n - w.shape[1])))


def _block_diag(blocks):
    n, d, e = blocks.shape
    eye = jnp.eye(n, dtype=blocks.dtype)
    return (eye[:, None, :, None] * blocks[:, :, None, :]).reshape(n * d, n * e)


def _mixers(z_parts, b, s, ml_gate_bias, ml_head_norm, lru, rw):
    zml, zlru, zrw, gcol, grow = z_parts
    zml = zml.reshape(b, s, -1)
    zlru = zlru.reshape(b, s, -1)
    zrw = zrw.reshape(b, s, -1)
    gcol = gcol.reshape(b, s, -1)
    bias = ml_gate_bias.reshape(1, 2 * ML_HEADS)
    yml = _mlstm(zml, gcol, grow, _pad_cols(bias, GATE_PAD), bias.reshape(-1, 1),
                 ml_head_norm.reshape(1, -1))
    ylru = _rglru(zlru, *lru)
    yrw = _rwkv(zrw, *rw)
    return yml, ylru, yrw


def kernel(x, p, norm_mix_pre, norm_mix_post, norm_ffn_pre, norm_ffn_post, norm_ple, w_in, w_out, ml_gate_bias, ml_head_norm, lru_conv_w, lru_conv_b, lru_w_r, lru_b_r, lru_w_i, lru_b_i, lru_lambda, rw_mu, rw_w0, rw_w2, rw_a0, rw_a2, rw_g2, rw_k_k, rw_k_a, rw_r_k, rw_ln_w, rw_ln_b, ffn_w1, ffn_w2, ple_w_proj, ple_w_gate):
    b, s, d = x.shape
    depth = w_in.shape[0]
    t = b * s
    ml_in = 4 * ML_DIM + 2 * ML_HEADS
    lru_in = 2 * LRU_DIM
    head_id = jnp.arange(RW_DIM) // HD
    hsum = (head_id[:, None] == head_id[None, :]).astype(BF16)
    row = lambda v: v.reshape(1, -1)

    x2 = x.reshape(t, d)
    for l in range(depth):
        wi = w_in[l]
        w_gate = wi[:, 4 * ML_DIM:ml_in]
        z_parts = _in_proj(
            x2, row(norm_mix_pre[l]),
            wi[:, :4 * ML_DIM].astype(BF16),
            wi[:, ml_in:ml_in + lru_in].astype(BF16),
            _pad_cols(wi[:, ml_in + lru_in:], RW_IN_PAD).astype(BF16),
            _pad_cols(w_gate, GATE_PAD).astype(BF16),
            w_gate.T.astype(BF16))

        lru = (lru_conv_w[l], row(lru_conv_b[l]),
               jnp.concatenate([_block_diag(lru_w_r[l]), _block_diag(lru_w_i[l])], axis=1).astype(BF16),
               jnp.concatenate([row(lru_b_r[l]), row(lru_b_i[l])], axis=1),
               row(lru_lambda[l]))

        w_lora = jnp.zeros((RW_LORA_PAD, 3 * RW_DIM), F32)
        w_lora = w_lora.at[0:32, 0:RW_DIM].set(rw_w2[l])
        w_lora = w_lora.at[32:64, RW_DIM:2 * RW_DIM].set(rw_a2[l])
        w_lora = w_lora.at[64:RW_LORA, 2 * RW_DIM:].set(rw_g2[l])
        rw = (_pad_cols(row(rw_mu[l]), RW_IN_PAD), w_lora.astype(BF16), row(rw_w0[l]), row(rw_a0[l]),
              row(rw_k_k[l]), row(rw_k_a[l]), row(rw_r_k[l]), row(rw_ln_w[l]), row(rw_ln_b[l]), hsum)

        yml, ylru, yrw = _mixers(z_parts, b, s, ml_gate_bias[l], ml_head_norm[l], lru, rw)
        x2 = _tail(yml.reshape(t, -1), ylru.reshape(t, -1), yrw.reshape(t, -1), x2, p[l].reshape(t, -1),
                   w_out[l].astype(BF16), row(norm_mix_post[l]), row(norm_ffn_pre[l]),
                   ffn_w1[l].astype(BF16), ffn_w2[l].astype(BF16), row(norm_ffn_post[l]),
                   ple_w_gate[l].astype(BF16), ple_w_proj[l].astype(BF16), row(norm_ple[l]))
    return x2.reshape(b, s, d)
```

```python
import functools

import jax
import jax.numpy as jnp
from jax import lax
from jax.experimental import pallas as pl
from jax.experimental.pallas import tpu as pltpu

F32 = jnp.float32
BF16 = jnp.bfloat16

EPS = 1e-6
HD = 64
ML_HEADS = 4
ML_DIM = ML_HEADS * HD
LRU_DIM = 256
LRU_C = 8.0
CONV_W = 4
LRU_GROUP = 16
RW_HEADS = 8
RW_DIM = RW_HEADS * HD
RW_LORA = 160
RW_LORA_PAD = 256
RW_IN_PAD = 3 * RW_DIM + RW_LORA_PAD
RW_GN_EPS = 64e-5
CHUNK = 64
GATE_PAD = 128

VMEM_LIMIT = 48 * 1024 * 1024

_NT = (((1,), (1,)), ((), ()))
_TN = (((0,), (0,)), ((), ()))


def _mm(a, b):
    return jnp.dot(a.astype(BF16), b.astype(BF16), preferred_element_type=F32)


def _mm_nt(a, b):
    return lax.dot_general(a.astype(BF16), b.astype(BF16), _NT, preferred_element_type=F32)


def _mm_tn(a, b):
    return lax.dot_general(a.astype(BF16), b.astype(BF16), _TN, preferred_element_type=F32)


def _mm_f32(a, b):
    return jnp.dot(a, b, preferred_element_type=F32, precision=lax.Precision.HIGHEST)


def _rms(x, gain):
    return x * lax.rsqrt(jnp.mean(x * x, axis=-1, keepdims=True) + EPS) * gain


def _softplus(x):
    return jnp.maximum(x, 0.0) + jnp.log(1.0 + jnp.exp(-jnp.abs(x)))


def _log_sigmoid(x):
    return jnp.minimum(x, 0.0) - jnp.log(1.0 + jnp.exp(-jnp.abs(x)))


def _split3(x):
    hi = x.astype(BF16)
    rest = x - hi.astype(F32)
    mid = rest.astype(BF16)
    return hi, mid, (rest - mid.astype(F32)).astype(BF16)


def _iota2(shape, dim):
    return lax.broadcasted_iota(jnp.int32, shape, dim)


def _params(*sem):
    return pltpu.CompilerParams(dimension_semantics=sem, vmem_limit_bytes=VMEM_LIMIT)


def _bd(y, left):
    return jnp.concatenate([jnp.where(left, y, 0.0), jnp.where(left, 0.0, y)], axis=0)


def _unbd(z, left):
    return jnp.where(left, z[0:HD, :], z[HD:2 * HD, :])


def _in_proj_kernel(x_ref, g_ref, wml_ref, wlru_ref, wrw_ref, wg_ref, wgt_ref,
                    zml_ref, zlru_ref, zrw_ref, gcol_ref, grow_ref):
    hb = _rms(x_ref[...], g_ref[...]).astype(BF16)
    zml_ref[...] = jnp.dot(hb, wml_ref[...], preferred_element_type=F32)
    zlru_ref[...] = jnp.dot(hb, wlru_ref[...], preferred_element_type=F32)
    zrw_ref[...] = jnp.dot(hb, wrw_ref[...], preferred_element_type=F32)
    gcol_ref[...] = jnp.dot(hb, wg_ref[...], preferred_element_type=F32)
    grow_ref[...] = lax.dot_general(wgt_ref[...], hb, _NT, preferred_element_type=F32)


def _in_proj(x2, gain, wml, wlru, wrw, wg, wgt, tm=512):
    t, d = x2.shape
    const = lambda i: (0, 0)
    row = lambda i: (i, 0)
    outs = (
        jax.ShapeDtypeStruct((t, wml.shape[1]), F32),
        jax.ShapeDtypeStruct((t, wlru.shape[1]), F32),
        jax.ShapeDtypeStruct((t, wrw.shape[1]), F32),
        jax.ShapeDtypeStruct((t, GATE_PAD), F32),
        jax.ShapeDtypeStruct((wgt.shape[0], t), F32),
    )
    return pl.pallas_call(
        _in_proj_kernel,
        grid=(t // tm,),
        in_specs=[
            pl.BlockSpec((tm, d), row),
            pl.BlockSpec((1, d), const),
            pl.BlockSpec(wml.shape, const),
            pl.BlockSpec(wlru.shape, const),
            pl.BlockSpec(wrw.shape, const),
            pl.BlockSpec(wg.shape, const),
            pl.BlockSpec(wgt.shape, const),
        ],
        out_specs=(
            pl.BlockSpec((tm, wml.shape[1]), row),
            pl.BlockSpec((tm, wlru.shape[1]), row),
            pl.BlockSpec((tm, wrw.shape[1]), row),
            pl.BlockSpec((tm, GATE_PAD), row),
            pl.BlockSpec((wgt.shape[0], tm), lambda i: (0, i)),
        ),
        out_shape=outs,
        compiler_params=_params("arbitrary"),
    )(x2, gain, wml, wlru, wrw, wg, wgt)


def _mlstm_kernel(z_ref, gcol_ref, grow_ref, bcol_ref, brow_ref, hn_ref, y_ref, st_sc, m_sc):
    L = CHUNK
    ts = z_ref.shape[0]
    nc = ts // L
    npair = ML_HEADS // 2
    pw = 2 * HD

    @pl.when(pl.program_id(1) == 0)
    def _():
        st_sc[...] = jnp.zeros_like(st_sc)
        m_sc[...] = jnp.zeros_like(m_sc)

    gc = gcol_ref[...] + bcol_ref[...]
    tri = (_iota2((L, L), 0) >= _iota2((L, L), 1)).astype(BF16)
    pieces = jnp.concatenate(_split3(_log_sigmoid(gc)), axis=1)
    f_col = []
    for c in range(nc):
        s3 = jnp.dot(tri, pieces[c * L:(c + 1) * L, :], preferred_element_type=F32)
        f_col.append(s3[:, 0:GATE_PAD] + s3[:, GATE_PAD:2 * GATE_PAD] + s3[:, 2 * GATE_PAD:])
    f_col = jnp.concatenate(f_col, axis=0)
    gr = grow_ref[...] + brow_ref[...]
    rt_i = _iota2((ts, ts), 0)
    ct_i = _iota2((ts, ts), 1)
    triu = ((rt_i <= ct_i) & ((rt_i // L) == (ct_i // L))).astype(BF16)
    s3 = jnp.dot(jnp.concatenate(_split3(_log_sigmoid(gr)), axis=0), triu, preferred_element_type=F32)
    ng = 2 * ML_HEADS
    f_row = s3[0:ng] + s3[ng:2 * ng] + s3[2 * ng:]
    g_rows = f_row[ML_HEADS:2 * ML_HEADS, :] - gr[0:ML_HEADS, :]
    g_up = pltpu.roll(g_rows, HD, 1)
    g_dn = pltpu.roll(g_rows, ts - HD, 1)

    t_i = _iota2((L, pw), 0)
    lane = _iota2((L, pw), 1)
    left = lane < HD
    left1 = left[0:1, :]
    causal = t_i >= (lane & (HD - 1))
    r2 = _iota2((pw, 2 * pw), 0)
    c2 = _iota2((pw, 2 * pw), 1)
    diag2 = (r2 // HD) == ((c2 // HD) % 2)
    top2 = r2 < HD
    ones_bd = diag2[:, 0:pw].astype(F32)
    ones_l = jnp.ones((L, pw), F32)
    scale = HD ** -0.5

    def pair_cols(arr, c, j0, j1):
        rows = arr[c * L:(c + 1) * L, :]
        return jnp.where(left, rows[:, j0:j0 + 1], rows[:, j1:j1 + 1])

    chains = [(p, c) for p in range(npair) for c in range(nc)]

    pre = []
    for p, c in chains:
        h0, h1 = 2 * p, 2 * p + 1
        rows = slice(c * L, (c + 1) * L)
        q = z_ref[rows, p * pw:(p + 1) * pw] * scale
        k = z_ref[rows, ML_DIM + p * pw:ML_DIM + (p + 1) * pw]
        v = z_ref[rows, 2 * ML_DIM + p * pw:2 * ML_DIM + (p + 1) * pw]
        fc = pair_cols(f_col, c, ML_HEADS + h0, ML_HEADS + h1)
        ic = pair_cols(gc, c, h0, h1)
        w0 = (c // 2) * pw
        if c % 2 == 0:
            g_pair = jnp.where(left1, g_rows[h0:h0 + 1, w0:w0 + pw], g_up[h1:h1 + 1, w0:w0 + pw])
        else:
            g_pair = jnp.where(left1, g_dn[h0:h0 + 1, w0:w0 + pw], g_rows[h1:h1 + 1, w0:w0 + pw])
        log_d = jnp.where(causal, fc - g_pair, -jnp.inf)
        d_max = jnp.where(left,
                          jnp.max(jnp.where(left, log_d, -jnp.inf), axis=-1, keepdims=True),
                          jnp.max(jnp.where(left, -jnp.inf, log_d), axis=-1, keepdims=True))
        s_raw = _mm_nt(q, _bd(k, left))
        f_tot = fc[L - 1:L, :]
        lw = f_tot - fc + ic
        m_loc = jnp.max(lw, axis=0, keepdims=True)
        kw = k * jnp.exp(lw - m_loc)
        new = _mm_tn(kw, jnp.concatenate([v, ones_l], axis=1))
        top = jnp.concatenate([_bd(v, left), ones_bd], axis=1)
        pre.append((q, fc, log_d, d_max, s_raw, f_tot, m_loc, new, top))

    lhs, rhs, m_rows = [], [], []
    for p in range(npair):
        state = st_sc[p]
        m_in = m_sc[p:p + 1, :]
        for c in range(nc):
            q, fc, log_d, d_max, s_raw, f_tot, m_loc, new, top = pre[p * nc + c]
            m_inter = fc + m_in
            m_row = jnp.maximum(m_inter, d_max)
            sd = s_raw * jnp.exp(log_d - m_row)
            lhs.append(jnp.concatenate([sd, jnp.exp(m_inter - m_row) * q], axis=1))
            rhs.append(jnp.concatenate([top, state], axis=0))
            m_rows.append(m_row)
            m_new = jnp.maximum(f_tot + m_in, m_loc)
            a = jnp.exp(f_tot + m_in - m_new)
            b = jnp.exp(m_loc - m_new)
            a_col = jnp.where(top2, a[:, 0:1], a[:, HD:HD + 1])
            b_col = jnp.where(top2, b[:, 0:1], b[:, HD:HD + 1])
            state = a_col * state + jnp.where(diag2, b_col * new, 0.0)
            m_in = m_new
        st_sc[p] = state
        m_sc[p:p + 1, :] = m_in

    for i, (p, c) in enumerate(chains):
        rows = slice(c * L, (c + 1) * L)
        nd = _mm(lhs[i], rhs[i])
        hh = nd[:, 0:pw] / jnp.maximum(jnp.abs(nd[:, pw:]), jnp.exp(-m_rows[i]))
        ms = _mm(hh * hh, ones_bd) * (1.0 / HD)
        o = z_ref[rows, 3 * ML_DIM + p * pw:3 * ML_DIM + (p + 1) * pw]
        y_ref[rows, p * pw:(p + 1) * pw] = ((hh * lax.rsqrt(ms + EPS) * hn_ref[:, p * pw:(p + 1) * pw])
                                            * jax.nn.sigmoid(o))


def _mlstm(zml, gcol, grow, bcol, brow, head_norm, ts=256):
    b, s, _ = zml.shape
    ts = min(ts, s)
    nblk = s // ts
    const = lambda i, j: (0, 0)
    return pl.pallas_call(
        _mlstm_kernel,
        grid=(b, nblk),
        in_specs=[
            pl.BlockSpec((None, ts, 4 * ML_DIM), lambda i, j: (i, j, 0)),
            pl.BlockSpec((None, ts, GATE_PAD), lambda i, j: (i, j, 0)),
            pl.BlockSpec((2 * ML_HEADS, ts), lambda i, j: (0, i * nblk + j)),
            pl.BlockSpec((1, GATE_PAD), const),
            pl.BlockSpec((2 * ML_HEADS, 1), const),
            pl.BlockSpec((1, ML_DIM), const),
        ],
        out_specs=pl.BlockSpec((None, ts, ML_DIM), lambda i, j: (i, j, 0)),
        out_shape=jax.ShapeDtypeStruct((b, s, ML_DIM), F32),
        scratch_shapes=[
            pltpu.VMEM((ML_HEADS // 2, 2 * HD, 4 * HD), F32),
            pltpu.VMEM((8, 2 * HD), F32),
        ],
        compiler_params=_params("arbitrary", "arbitrary"),
    )(zml, gcol, grow, bcol, brow, head_norm)


def _shift_rows(x, d, fill, row):
    return jnp.where(row >= d, pltpu.roll(x, d, 0), fill)


def _lru_kernel(z_ref, cw_ref, cb_ref, wg_ref, bg_ref, lam_ref, y_ref, xbuf, h_sc):
    ts = z_ref.shape[0]

    @pl.when(pl.program_id(1) == 0)
    def _():
        xbuf[0:8, :] = jnp.zeros((8, LRU_DIM), F32)
        h_sc[...] = jnp.zeros_like(h_sc)

    xb = z_ref[:, 0:LRU_DIM]
    gb = z_ref[:, LRU_DIM:2 * LRU_DIM]
    xbuf[8:8 + ts, :] = xb
    xc = cb_ref[...] + cw_ref[CONV_W - 1:CONV_W, :] * xb
    for j in range(1, CONV_W):
        xc = xc + cw_ref[CONV_W - 1 - j:CONV_W - j, :] * xbuf[8 - j:8 - j + ts, :]
    xbuf[0:8, :] = xb[ts - 8:ts, :]

    gates = _mm(xc, wg_ref[...]) + bg_ref[...]
    r = jax.nn.sigmoid(gates[:, 0:LRU_DIM])
    i = jax.nn.sigmoid(gates[:, LRU_DIM:2 * LRU_DIM])
    log_a = -LRU_C * r * _softplus(-lam_ref[...])
    a = jnp.exp(log_a)
    u = jnp.sqrt(-jnp.tanh(log_a) * (a * a + 1.0)) * (i * xc)

    row = _iota2((ts, LRU_DIM), 0) & (LRU_GROUP - 1)
    acc_a, acc_h = a, u
    d = 1
    while d < LRU_GROUP:
        acc_h = acc_a * _shift_rows(acc_h, d, 0.0, row) + acc_h
        acc_a = acc_a * _shift_rows(acc_a, d, 1.0, row)
        d *= 2
    gelu = 0.5 * gb * (1.0 + jnp.tanh(0.7978845608028654 * (gb + 0.044715 * (gb * gb * gb))))
    carry = h_sc[0:1, :]
    for g in range(ts // LRU_GROUP):
        rows = slice(g * LRU_GROUP, (g + 1) * LRU_GROUP)
        hg = acc_a[rows, :] * carry + acc_h[rows, :]
        y_ref[rows, :] = hg * gelu[rows, :]
        carry = hg[LRU_GROUP - 1:LRU_GROUP, :]
    h_sc[...] = jnp.broadcast_to(carry, h_sc.shape)


def _rglru(zlru, conv_w, conv_b, wg, bg, lam, ts=512):
    b, s, _ = zlru.shape
    ts = min(ts, s)
    const = lambda i, j: (0, 0)
    return pl.pallas_call(
        _lru_kernel,
        grid=(b, s // ts),
        in_specs=[
            pl.BlockSpec((None, ts, 2 * LRU_DIM), lambda i, j: (i, j, 0)),
            pl.BlockSpec((CONV_W, LRU_DIM), const),
            pl.BlockSpec((1, LRU_DIM), const),
            pl.BlockSpec((LRU_DIM, 2 * LRU_DIM), const),
            pl.BlockSpec((1, 2 * LRU_DIM), const),
            pl.BlockSpec((1, LRU_DIM), const),
        ],
        out_specs=pl.BlockSpec((None, ts, LRU_DIM), lambda i, j: (i, j, 0)),
        out_shape=jax.ShapeDtypeStruct((b, s, LRU_DIM), F32),
        scratch_shapes=[
            pltpu.VMEM((ts + 8, LRU_DIM), F32),
            pltpu.VMEM((8, LRU_DIM), F32),
        ],
        compiler_params=_params("arbitrary", "arbitrary"),
    )(zlru, conv_w, conv_b, wg, bg, lam)


def _rwkv_kernel(z_ref, mu_ref, wl_ref, w0_ref, a0_ref, kk_ref, ka_ref, rk_ref, lnw_ref, lnb_ref,
                 hsum_ref, y_ref, prev_sc, p_sc, o_sc):
    L = CHUNK
    ts = z_ref.shape[0]
    nc = ts // L
    npair = RW_HEADS // 2

    @pl.when(pl.program_id(1) == 0)
    def _():
        prev_sc[...] = jnp.zeros_like(prev_sc)
        p_sc[...] = jnp.zeros_like(p_sc)

    z = z_ref[...]
    row = _iota2(z.shape, 0)
    zprev = jnp.where(row == 0, prev_sc[0:1, :], pltpu.roll(z, 1, 0))
    prev_sc[...] = jnp.broadcast_to(z[ts - 1:ts, :], prev_sc.shape)
    z = z + (zprev - z) * mu_ref[...]

    r = z[:, 0:RW_DIM]
    k = z[:, RW_DIM:2 * RW_DIM]
    v = z[:, 2 * RW_DIM:3 * RW_DIM]
    lo = z[:, 3 * RW_DIM:3 * RW_DIM + RW_LORA_PAD]
    lane = _iota2(lo.shape, 1)
    lo = jnp.where(lane < 32, jnp.tanh(lo), jnp.where(lane < 64, lo, jax.nn.sigmoid(lo)))
    lora = _mm(lo, wl_ref[...])
    w = -_softplus(-(w0_ref[...] + lora[:, 0:RW_DIM])) - 0.5
    log_w = -jnp.exp(w)
    a = jax.nn.sigmoid(a0_ref[...] + lora[:, RW_DIM:2 * RW_DIM])
    g = lora[:, 2 * RW_DIM:3 * RW_DIM]

    hsum = hsum_ref[...]
    kk = k * kk_ref[...]
    kk = kk / jnp.maximum(jnp.sqrt(_mm(kk * kk, hsum)), 1e-12)
    k = k * (1.0 + (a - 1.0) * ka_ref[...])
    bonus = _mm(r * k * rk_ref[...], hsum) * v
    b = kk * a

    tri = (_iota2((L, L), 0) >= _iota2((L, L), 1)).astype(BF16)
    pieces = jnp.concatenate(_split3(log_w), axis=1)
    cum = []
    for c in range(nc):
        s3 = jnp.dot(tri, pieces[c * L:(c + 1) * L, :], preferred_element_type=F32)
        cum.append(s3[:, 0:RW_DIM] + s3[:, RW_DIM:2 * RW_DIM] + s3[:, 2 * RW_DIM:])
    cum = jnp.concatenate(cum, axis=0)
    g_inv = jnp.exp(-cum)
    r_t = r * jnp.exp(cum)
    k_t = k * g_inv
    b_t = b * g_inv
    kp_t = kk * jnp.exp(cum - log_w)

    t_i = _iota2((L, 2 * HD), 0)
    lane = _iota2((L, 2 * HD), 1)
    s_i = lane & (HD - 1)
    left = lane < HD
    lower = t_i >= s_i
    strict = t_i > s_i
    eye = (t_i == s_i).astype(F32)
    levels = []
    bs = 1
    while bs < L:
        shift = bs.bit_length() - 1
        x, y = t_i >> shift, s_i >> shift
        levels.append(((x ^ y) == 1) & (x > y))
        bs *= 2

    chains = [(c, p) for c in range(nc) for p in range(npair)]
    cut = lambda arr, c, p: arr[c * L:(c + 1) * L, p * 2 * HD:(p + 1) * 2 * HD]

    rt, kpt, vv, qk, qb, ak, ab = [], [], [], [], [], [], []
    for c, p in chains:
        rt.append(cut(r_t, c, p))
        kpt.append(cut(kp_t, c, p))
        vv.append(cut(v, c, p))
        rhs = jnp.concatenate([_bd(cut(k_t, c, p), left), _bd(cut(b_t, c, p), left)], axis=0)
        sc = _mm_nt(jnp.concatenate([rt[-1], kpt[-1]], axis=0), rhs)
        qk.append(jnp.where(lower, sc[0:L, 0:2 * HD], 0.0))
        qb.append(jnp.where(lower, sc[0:L, 2 * HD:], 0.0))
        ak.append(jnp.where(strict, sc[L:, 0:2 * HD], 0.0))
        ab.append(jnp.where(strict, sc[L:, 2 * HD:], 0.0))

    tinv = [eye - jnp.where(levels[0], a, 0.0) for a in ab]
    for lvl in levels[1:]:
        xs = [_mm(jnp.where(lvl, a, 0.0), _bd(t, left)) for a, t in zip(ab, tinv)]
        tinv = [t - _mm(t, _bd(x, left)) for t, x in zip(tinv, xs)]

    qa = [_mm(jnp.concatenate([q, a], axis=0), _bd(v_, left)) for q, a, v_ in zip(qk, ak, vv)]
    ww = [_mm(t, jnp.concatenate([_bd(kp, left), _bd(q[L:], left)], axis=1))
          for t, kp, q in zip(tinv, kpt, qa)]
    qw = [_mm(q, jnp.concatenate([_bd(w[:, 0:2 * HD], left), _bd(w[:, 2 * HD:], left)], axis=1))
          for q, w in zip(qb, ww)]
    r_hat = [r_ - q[:, 0:2 * HD] for r_, q in zip(rt, qw)]
    o_loc = [q[0:L] - w[:, 2 * HD:] for q, w in zip(qa, qw)]

    m_st, n_st = [], []
    for i, (c, p) in enumerate(chains):
        cum_c = cut(cum, c, p)
        cum_end = cum_c[L - 1:L, :]
        g_end = jnp.exp(cum_end - cum_c)
        g1 = _mm_tn(cut(b, c, p) * g_end, ww[i])
        g2 = _mm_tn(cut(k, c, p) * g_end, vv[i])
        m_st.append(eye * jnp.exp(cum_end) - _unbd(g1[:, 0:2 * HD], left))
        n_st.append(_unbd(g2, left) - _unbd(g1[:, 2 * HD:], left))

    state = [p_sc[p] for p in range(npair)]
    for i, (c, p) in enumerate(chains):
        op = _mm(jnp.concatenate([r_hat[i], m_st[i]], axis=0), _bd(state[p], left))
        o_sc[c * L:(c + 1) * L, p * 2 * HD:(p + 1) * 2 * HD] = op[0:L] + o_loc[i]
        state[p] = op[L:] + n_st[i]
    for p in range(npair):
        p_sc[p] = state[p]
    o = o_sc[...]

    mean = _mm(o, hsum) * (1.0 / HD)
    dev = o - mean
    var = _mm(dev * dev, hsum) * (1.0 / HD)
    o = dev * lax.rsqrt(var + RW_GN_EPS) * lnw_ref[...] + lnb_ref[...]
    y_ref[...] = (o + bonus) * g


def _rwkv(zrw, mu, wl, w0, a0, k_k, k_a, r_k, ln_w, ln_b, hsum, ts=256):
    b, s, _ = zrw.shape
    ts = min(ts, s)
    const = lambda i, j: (0, 0)
    vec = pl.BlockSpec((1, RW_DIM), const)
    return pl.pallas_call(
        _rwkv_kernel,
        grid=(b, s // ts),
        in_specs=[
            pl.BlockSpec((None, ts, RW_IN_PAD), lambda i, j: (i, j, 0)),
            pl.BlockSpec((1, RW_IN_PAD), const),
            pl.BlockSpec((RW_LORA_PAD, 3 * RW_DIM), const),
            vec, vec, vec, vec, vec, vec, vec,
            pl.BlockSpec((RW_DIM, RW_DIM), const),
        ],
        out_specs=pl.BlockSpec((None, ts, RW_DIM), lambda i, j: (i, j, 0)),
        out_shape=jax.ShapeDtypeStruct((b, s, RW_DIM), F32),
        scratch_shapes=[
            pltpu.VMEM((8, RW_IN_PAD), F32),
            pltpu.VMEM((RW_HEADS // 2, HD, 2 * HD), F32),
            pltpu.VMEM((ts, RW_DIM), F32),
        ],
        compiler_params=_params("arbitrary", "arbitrary"),
    )(zrw, mu, wl, w0, a0, k_k, k_a, r_k, ln_w, ln_b, hsum)


def _tail_kernel(yml_ref, ylru_ref, yrw_ref, x_ref, p_ref, wo_ref, gmix_ref, gpre_ref, w1_ref, w2_ref,
                 gffn_ref, wg_ref, wp_ref, gple_ref, o_ref, *, tf):
    mix = (_mm(yml_ref[...], wo_ref[0:ML_DIM, :])
           + _mm(ylru_ref[...], wo_ref[ML_DIM:ML_DIM + LRU_DIM, :])
           + _mm(yrw_ref[...], wo_ref[ML_DIM + LRU_DIM:, :]))
    x1 = x_ref[...] + _rms(mix, gmix_ref[...])
    h = _rms(x1, gpre_ref[...]).astype(BF16)
    acc = None
    for c in range(w1_ref.shape[1] // tf):
        u = jnp.maximum(jnp.dot(h, w1_ref[:, c * tf:(c + 1) * tf], preferred_element_type=F32), 0.0)
        part = _mm(u * u, w2_ref[c * tf:(c + 1) * tf, :])
        acc = part if acc is None else acc + part
    x2 = x1 + _rms(acc, gffn_ref[...])
    e = jax.nn.sigmoid(_mm(x2, wg_ref[...])) * _mm(p_ref[...], wp_ref[...])
    o_ref[...] = x2 + _rms(e, gple_ref[...])


def _tail(yml, ylru, yrw, x2, p_all, layer, wo, gmix, gpre, w1, w2, gffn, wg, wp, gple, tm=512, tf=1024):
    t, d = x2.shape
    dp = p_all.shape[1]
    const = lambda i: (0, 0)
    row = lambda i: (i, 0)
    p_row = lambda i: (layer * (t // tm) + i, 0)
    resident = lambda w: pl.BlockSpec(w.shape, const, pipeline_mode=pl.Buffered(1))
    vec = pl.BlockSpec((1, d), const)
    return pl.pallas_call(
        functools.partial(_tail_kernel, tf=tf),
        grid=(t // tm,),
        in_specs=[
            pl.BlockSpec((tm, ML_DIM), row),
            pl.BlockSpec((tm, LRU_DIM), row),
            pl.BlockSpec((tm, RW_DIM), row),
            pl.BlockSpec((tm, d), row),
            pl.BlockSpec((tm, dp), p_row),
            resident(wo), vec, vec, resident(w1), resident(w2), vec, resident(wg), resident(wp), vec,
        ],
        out_specs=pl.BlockSpec((tm, d), row),
        out_shape=jax.ShapeDtypeStruct((t, d), F32),
        compiler_params=_params("arbitrary"),
    )(yml, ylru, yrw, x2, p_all, wo, gmix, gpre, w1, w2, gffn, wg, wp, gple)


def _pad_cols(w, n):
    return jnp.pad(w, ((0, 0), (0, n - w.shape[1])))


def _block_diag(blocks):
    n, d, e = blocks.shape
    eye = jnp.eye(n, dtype=blocks.dtype)
    return (eye[:, None, :, None] * blocks[:, :, None, :]).reshape(n * d, n * e)


def _mixers(z_parts, b, s, ml_gate_bias, ml_head_norm, lru, rw):
    zml, zlru, zrw, gcol, grow = z_parts
    zml = zml.reshape(b, s, -1)
    zlru = zlru.reshape(b, s, -1)
    zrw = zrw.reshape(b, s, -1)
    gcol = gcol.reshape(b, s, -1)
    bias = ml_gate_bias.reshape(1, 2 * ML_HEADS)
    yml = _mlstm(zml, gcol, grow, _pad_cols(bias, GATE_PAD), bias.reshape(-1, 1),
                 ml_head_norm.reshape(1, -1))
    ylru = _rglru(zlru, *lru)
    yrw = _rwkv(zrw, *rw)
    return yml, ylru, yrw


def kernel(x, p, norm_mix_pre, norm_mix_post, norm_ffn_pre, norm_ffn_post, norm_ple, w_in, w_out, ml_gate_bias, ml_head_norm, lru_conv_w, lru_conv_b, lru_w_r, lru_b_r, lru_w_i, lru_b_i, lru_lambda, rw_mu, rw_w0, rw_w2, rw_a0, rw_a2, rw_g2, rw_k_k, rw_k_a, rw_r_k, rw_ln_w, rw_ln_b, ffn_w1, ffn_w2, ple_w_proj, ple_w_gate):
    b, s, d = x.shape
    depth = w_in.shape[0]
    t = b * s
    ml_in = 4 * ML_DIM + 2 * ML_HEADS
    lru_in = 2 * LRU_DIM
    head_id = jnp.arange(RW_DIM) // HD
    hsum = (head_id[:, None] == head_id[None, :]).astype(BF16)
    row = lambda v: v.reshape(1, -1)

    x2 = x.reshape(t, d)
    for l in range(depth):
        wi = w_in[l]
        w_gate = wi[:, 4 * ML_DIM:ml_in]
        z_parts = _in_proj(
            x2, row(norm_mix_pre[l]),
            wi[:, :4 * ML_DIM].astype(BF16),
            wi[:, ml_in:ml_in + lru_in].astype(BF16),
            _pad_cols(wi[:, ml_in + lru_in:], RW_IN_PAD).astype(BF16),
            _pad_cols(w_gate, GATE_PAD).astype(BF16),
            w_gate.T.astype(BF16))

        lru = (lru_conv_w[l], row(lru_conv_b[l]),
               jnp.concatenate([_block_diag(lru_w_r[l]), _block_diag(lru_w_i[l])], axis=1).astype(BF16),
               jnp.concatenate([row(lru_b_r[l]), row(lru_b_i[l])], axis=1),
               row(lru_lambda[l]))

        w_lora = jnp.zeros((RW_LORA_PAD, 3 * RW_DIM), F32)
        w_lora = w_lora.at[0:32, 0:RW_DIM].set(rw_w2[l])
        w_lora = w_lora.at[32:64, RW_DIM:2 * RW_DIM].set(rw_a2[l])
        w_lora = w_lora.at[64:RW_LORA, 2 * RW_DIM:].set(rw_g2[l])
        rw = (_pad_cols(row(rw_mu[l]), RW_IN_PAD), w_lora.astype(BF16), row(rw_w0[l]), row(rw_a0[l]),
              row(rw_k_k[l]), row(rw_k_a[l]), row(rw_r_k[l]), row(rw_ln_w[l]), row(rw_ln_b[l]), hsum)

        yml, ylru, yrw = _mixers(z_parts, b, s, ml_gate_bias[l], ml_head_norm[l], lru, rw)
        x2 = _tail(yml.reshape(t, -1), ylru.reshape(t, -1), yrw.reshape(t, -1), x2,
                   p.reshape(depth * t, -1), l, w_out[l].astype(BF16), row(norm_mix_post[l]), row(norm_ffn_pre[l]),
                   ffn_w1[l].astype(BF16), ffn_w2[l].astype(BF16), row(norm_ffn_post[l]),
                   ple_w_gate[l].astype(BF16), ple_w_proj[l].astype(BF16), row(norm_ple[l]))
    return x2.reshape(b, s, d)
```

```python
import functools

import jax
import jax.numpy as jnp
from jax import lax
from jax.experimental import pallas as pl
from jax.experimental.pallas import tpu as pltpu

F32 = jnp.float32
BF16 = jnp.bfloat16

EPS = 1e-6
HD = 64
ML_HEADS = 4
ML_DIM = ML_HEADS * HD
LRU_DIM = 256
LRU_C = 8.0
CONV_W = 4
LRU_GROUP = 16
RW_HEADS = 8
RW_DIM = RW_HEADS * HD
RW_LORA = 160
RW_LORA_PAD = 256
RW_IN_PAD = 3 * RW_DIM + RW_LORA_PAD
RW_GN_EPS = 64e-5
CHUNK = 64
GATE_PAD = 128

VMEM_LIMIT = 48 * 1024 * 1024
MXU_WIDTH = 256

_NT = (((1,), (1,)), ((), ()))
_TN = (((0,), (0,)), ((), ()))


def _mm(a, b):
    return jnp.dot(a.astype(BF16), b.astype(BF16), preferred_element_type=F32)


def _mm_nt(a, b):
    return lax.dot_general(a.astype(BF16), b.astype(BF16), _NT, preferred_element_type=F32)


def _mm_tn(a, b):
    return lax.dot_general(a.astype(BF16), b.astype(BF16), _TN, preferred_element_type=F32)


def _mm_f32(a, b):
    return jnp.dot(a, b, preferred_element_type=F32, precision=lax.Precision.HIGHEST)


def _rms(x, gain):
    return x * lax.rsqrt(jnp.mean(x * x, axis=-1, keepdims=True) + EPS) * gain


def _softplus(x):
    return jnp.maximum(x, 0.0) + jnp.log(1.0 + jnp.exp(-jnp.abs(x)))


def _log_sigmoid(x):
    return jnp.minimum(x, 0.0) - jnp.log(1.0 + jnp.exp(-jnp.abs(x)))


def _split3(x):
    hi = x.astype(BF16)
    rest = x - hi.astype(F32)
    mid = rest.astype(BF16)
    return hi, mid, (rest - mid.astype(F32)).astype(BF16)


def _iota2(shape, dim):
    return lax.broadcasted_iota(jnp.int32, shape, dim)


def _params(*sem):
    return pltpu.CompilerParams(dimension_semantics=sem, vmem_limit_bytes=VMEM_LIMIT)


def _bd(y, left):
    return jnp.concatenate([jnp.where(left, y, 0.0), jnp.where(left, 0.0, y)], axis=0)


def _unbd(z, left):
    return jnp.where(left, z[0:HD, :], z[HD:2 * HD, :])


def _in_proj_kernel(x_ref, g_ref, wml_ref, wlru_ref, wrw_ref, wg_ref, wgt_ref,
                    zml_ref, zlru_ref, zrw_ref, gcol_ref, grow_ref):
    hb = _rms(x_ref[...], g_ref[...]).astype(BF16)
    zml_ref[...] = jnp.dot(hb, wml_ref[...], preferred_element_type=F32)
    zlru_ref[...] = jnp.dot(hb, wlru_ref[...], preferred_element_type=F32)
    zrw_ref[...] = jnp.dot(hb, wrw_ref[...], preferred_element_type=F32)
    gcol_ref[...] = jnp.dot(hb, wg_ref[...], preferred_element_type=F32)
    grow_ref[...] = lax.dot_general(wgt_ref[...], hb, _NT, preferred_element_type=F32)


def _in_proj(x2, gain, wml, wlru, wrw, wg, wgt, tm=512):
    t, d = x2.shape
    const = lambda i: (0, 0)
    row = lambda i: (i, 0)
    outs = (
        jax.ShapeDtypeStruct((t, wml.shape[1]), F32),
        jax.ShapeDtypeStruct((t, wlru.shape[1]), F32),
        jax.ShapeDtypeStruct((t, wrw.shape[1]), F32),
        jax.ShapeDtypeStruct((t, GATE_PAD), F32),
        jax.ShapeDtypeStruct((wgt.shape[0], t), F32),
    )
    return pl.pallas_call(
        _in_proj_kernel,
        grid=(t // tm,),
        in_specs=[
            pl.BlockSpec((tm, d), row),
            pl.BlockSpec((1, d), const),
            pl.BlockSpec(wml.shape, const),
            pl.BlockSpec(wlru.shape, const),
            pl.BlockSpec(wrw.shape, const),
            pl.BlockSpec(wg.shape, const),
            pl.BlockSpec(wgt.shape, const),
        ],
        out_specs=(
            pl.BlockSpec((tm, wml.shape[1]), row),
            pl.BlockSpec((tm, wlru.shape[1]), row),
            pl.BlockSpec((tm, wrw.shape[1]), row),
            pl.BlockSpec((tm, GATE_PAD), row),
            pl.BlockSpec((wgt.shape[0], tm), lambda i: (0, i)),
        ),
        out_shape=outs,
        compiler_params=_params("arbitrary"),
    )(x2, gain, wml, wlru, wrw, wg, wgt)


def _mlstm_kernel(z_ref, gcol_ref, grow_ref, bcol_ref, brow_ref, hn_ref, y_ref, st_sc, m_sc):
    L = CHUNK
    ts = z_ref.shape[0]
    nc = ts // L
    npair = ML_HEADS // 2
    pw = 2 * HD

    @pl.when(pl.program_id(1) == 0)
    def _():
        st_sc[...] = jnp.zeros_like(st_sc)
        m_sc[...] = jnp.zeros_like(m_sc)

    gc = gcol_ref[...] + bcol_ref[...]
    tri = (_iota2((L, L), 0) >= _iota2((L, L), 1)).astype(BF16)
    pieces = jnp.concatenate(_split3(_log_sigmoid(gc)), axis=1)
    f_col = []
    for c in range(nc):
        s3 = jnp.dot(tri, pieces[c * L:(c + 1) * L, :], preferred_element_type=F32)
        f_col.append(s3[:, 0:GATE_PAD] + s3[:, GATE_PAD:2 * GATE_PAD] + s3[:, 2 * GATE_PAD:])
    f_col = jnp.concatenate(f_col, axis=0)
    gr = grow_ref[...] + brow_ref[...]
    rt_i = _iota2((ts, ts), 0)
    ct_i = _iota2((ts, ts), 1)
    triu = ((rt_i <= ct_i) & ((rt_i // L) == (ct_i // L))).astype(BF16)
    s3 = jnp.dot(jnp.concatenate(_split3(_log_sigmoid(gr)), axis=0), triu, preferred_element_type=F32)
    ng = 2 * ML_HEADS
    f_row = s3[0:ng] + s3[ng:2 * ng] + s3[2 * ng:]
    g_rows = f_row[ML_HEADS:2 * ML_HEADS, :] - gr[0:ML_HEADS, :]
    g_up = pltpu.roll(g_rows, HD, 1)
    g_dn = pltpu.roll(g_rows, ts - HD, 1)

    t_i = _iota2((L, pw), 0)
    lane = _iota2((L, pw), 1)
    left = lane < HD
    left1 = left[0:1, :]
    causal = t_i >= (lane & (HD - 1))
    r2 = _iota2((pw, 2 * pw), 0)
    c2 = _iota2((pw, 2 * pw), 1)
    diag2 = (r2 // HD) == ((c2 // HD) % 2)
    top2 = r2 < HD
    ones_bd = diag2[:, 0:pw].astype(F32)
    ones_l = jnp.ones((L, pw), F32)
    scale = HD ** -0.5

    def pair_cols(arr, c, j0, j1):
        rows = arr[c * L:(c + 1) * L, :]
        return jnp.where(left, rows[:, j0:j0 + 1], rows[:, j1:j1 + 1])

    chains = [(p, c) for p in range(npair) for c in range(nc)]

    pre = []
    for p, c in chains:
        h0, h1 = 2 * p, 2 * p + 1
        rows = slice(c * L, (c + 1) * L)
        q = z_ref[rows, p * pw:(p + 1) * pw] * scale
        k = z_ref[rows, ML_DIM + p * pw:ML_DIM + (p + 1) * pw]
        v = z_ref[rows, 2 * ML_DIM + p * pw:2 * ML_DIM + (p + 1) * pw]
        fc = pair_cols(f_col, c, ML_HEADS + h0, ML_HEADS + h1)
        ic = pair_cols(gc, c, h0, h1)
        w0 = (c // 2) * pw
        if c % 2 == 0:
            g_pair = jnp.where(left1, g_rows[h0:h0 + 1, w0:w0 + pw], g_up[h1:h1 + 1, w0:w0 + pw])
        else:
            g_pair = jnp.where(left1, g_dn[h0:h0 + 1, w0:w0 + pw], g_rows[h1:h1 + 1, w0:w0 + pw])
        log_d = jnp.where(causal, fc - g_pair, -jnp.inf)
        d_max = jnp.where(left,
                          jnp.max(jnp.where(left, log_d, -jnp.inf), axis=-1, keepdims=True),
                          jnp.max(jnp.where(left, -jnp.inf, log_d), axis=-1, keepdims=True))
        s_raw = _mm_nt(q, _bd(k, left))
        f_tot = fc[L - 1:L, :]
        lw = f_tot - fc + ic
        m_loc = jnp.max(lw, axis=0, keepdims=True)
        kw = k * jnp.exp(lw - m_loc)
        new = _mm_tn(kw, jnp.concatenate([v, ones_l], axis=1))
        top = jnp.concatenate([_bd(v, left), ones_bd], axis=1)
        pre.append((q, fc, log_d, d_max, s_raw, f_tot, m_loc, new, top))

    lhs, rhs, m_rows = [], [], []
    for p in range(npair):
        state = st_sc[p]
        m_in = m_sc[p:p + 1, :]
        for c in range(nc):
            q, fc, log_d, d_max, s_raw, f_tot, m_loc, new, top = pre[p * nc + c]
            m_inter = fc + m_in
            m_row = jnp.maximum(m_inter, d_max)
            sd = s_raw * jnp.exp(log_d - m_row)
            lhs.append(jnp.concatenate([sd, jnp.exp(m_inter - m_row) * q], axis=1))
            rhs.append(jnp.concatenate([top, state], axis=0))
            m_rows.append(m_row)
            m_new = jnp.maximum(f_tot + m_in, m_loc)
            a = jnp.exp(f_tot + m_in - m_new)
            b = jnp.exp(m_loc - m_new)
            a_col = jnp.where(top2, a[:, 0:1], a[:, HD:HD + 1])
            b_col = jnp.where(top2, b[:, 0:1], b[:, HD:HD + 1])
            state = a_col * state + jnp.where(diag2, b_col * new, 0.0)
            m_in = m_new
        st_sc[p] = state
        m_sc[p:p + 1, :] = m_in

    for i, (p, c) in enumerate(chains):
        rows = slice(c * L, (c + 1) * L)
        nd = _mm(lhs[i], rhs[i])
        hh = nd[:, 0:pw] / jnp.maximum(jnp.abs(nd[:, pw:]), jnp.exp(-m_rows[i]))
        ms = _mm(hh * hh, ones_bd) * (1.0 / HD)
        o = z_ref[rows, 3 * ML_DIM + p * pw:3 * ML_DIM + (p + 1) * pw]
        y_ref[rows, p * pw:(p + 1) * pw] = ((hh * lax.rsqrt(ms + EPS) * hn_ref[:, p * pw:(p + 1) * pw])
                                            * jax.nn.sigmoid(o))


def _mlstm(zml, gcol, grow, bcol, brow, head_norm, ts=256):
    b, s, _ = zml.shape
    ts = min(ts, s)
    nblk = s // ts
    const = lambda i, j: (0, 0)
    return pl.pallas_call(
        _mlstm_kernel,
        grid=(b, nblk),
        in_specs=[
            pl.BlockSpec((None, ts, 4 * ML_DIM), lambda i, j: (i, j, 0)),
            pl.BlockSpec((None, ts, GATE_PAD), lambda i, j: (i, j, 0)),
            pl.BlockSpec((2 * ML_HEADS, ts), lambda i, j: (0, i * nblk + j)),
            pl.BlockSpec((1, GATE_PAD), const),
            pl.BlockSpec((2 * ML_HEADS, 1), const),
            pl.BlockSpec((1, ML_DIM), const),
        ],
        out_specs=pl.BlockSpec((None, ts, ML_DIM), lambda i, j: (i, j, 0)),
        out_shape=jax.ShapeDtypeStruct((b, s, ML_DIM), F32),
        scratch_shapes=[
            pltpu.VMEM((ML_HEADS // 2, 2 * HD, 4 * HD), F32),
            pltpu.VMEM((8, 2 * HD), F32),
        ],
        compiler_params=_params("arbitrary", "arbitrary"),
    )(zml, gcol, grow, bcol, brow, head_norm)


def _shift_rows(x, d, fill, row):
    return jnp.where(row >= d, pltpu.roll(x, d, 0), fill)


def _lru_kernel(z_ref, cw_ref, cb_ref, wg_ref, bg_ref, lam_ref, y_ref, xbuf, h_sc):
    ts = z_ref.shape[0]

    @pl.when(pl.program_id(1) == 0)
    def _():
        xbuf[0:8, :] = jnp.zeros((8, LRU_DIM), F32)
        h_sc[...] = jnp.zeros_like(h_sc)

    xb = z_ref[:, 0:LRU_DIM]
    gb = z_ref[:, LRU_DIM:2 * LRU_DIM]
    xbuf[8:8 + ts, :] = xb
    xc = cb_ref[...] + cw_ref[CONV_W - 1:CONV_W, :] * xb
    for j in range(1, CONV_W):
        xc = xc + cw_ref[CONV_W - 1 - j:CONV_W - j, :] * xbuf[8 - j:8 - j + ts, :]
    xbuf[0:8, :] = xb[ts - 8:ts, :]

    gates = _mm(xc, wg_ref[...]) + bg_ref[...]
    r = jax.nn.sigmoid(gates[:, 0:LRU_DIM])
    i = jax.nn.sigmoid(gates[:, LRU_DIM:2 * LRU_DIM])
    log_a = -LRU_C * r * _softplus(-lam_ref[...])
    a = jnp.exp(log_a)
    u = jnp.sqrt(-jnp.tanh(log_a) * (a * a + 1.0)) * (i * xc)

    row = _iota2((ts, LRU_DIM), 0) & (LRU_GROUP - 1)
    acc_a, acc_h = a, u
    d = 1
    while d < LRU_GROUP:
        acc_h = acc_a * _shift_rows(acc_h, d, 0.0, row) + acc_h
        acc_a = acc_a * _shift_rows(acc_a, d, 1.0, row)
        d *= 2
    gelu = 0.5 * gb * (1.0 + jnp.tanh(0.7978845608028654 * (gb + 0.044715 * (gb * gb * gb))))
    carry = h_sc[0:1, :]
    for g in range(ts // LRU_GROUP):
        rows = slice(g * LRU_GROUP, (g + 1) * LRU_GROUP)
        hg = acc_a[rows, :] * carry + acc_h[rows, :]
        y_ref[rows, :] = hg * gelu[rows, :]
        carry = hg[LRU_GROUP - 1:LRU_GROUP, :]
    h_sc[...] = jnp.broadcast_to(carry, h_sc.shape)


def _rglru(zlru, conv_w, conv_b, wg, bg, lam, ts=512):
    b, s, _ = zlru.shape
    ts = min(ts, s)
    const = lambda i, j: (0, 0)
    return pl.pallas_call(
        _lru_kernel,
        grid=(b, s // ts),
        in_specs=[
            pl.BlockSpec((None, ts, 2 * LRU_DIM), lambda i, j: (i, j, 0)),
            pl.BlockSpec((CONV_W, LRU_DIM), const),
            pl.BlockSpec((1, LRU_DIM), const),
            pl.BlockSpec((LRU_DIM, 2 * LRU_DIM), const),
            pl.BlockSpec((1, 2 * LRU_DIM), const),
            pl.BlockSpec((1, LRU_DIM), const),
        ],
        out_specs=pl.BlockSpec((None, ts, LRU_DIM), lambda i, j: (i, j, 0)),
        out_shape=jax.ShapeDtypeStruct((b, s, LRU_DIM), F32),
        scratch_shapes=[
            pltpu.VMEM((ts + 8, LRU_DIM), F32),
            pltpu.VMEM((8, LRU_DIM), F32),
        ],
        compiler_params=_params("arbitrary", "arbitrary"),
    )(zlru, conv_w, conv_b, wg, bg, lam)


def _rwkv_kernel(z_ref, mu_ref, wl_ref, w0_ref, a0_ref, kk_ref, ka_ref, rk_ref, lnw_ref, lnb_ref,
                 hsum_ref, y_ref, prev_sc, p_sc, o_sc):
    L = CHUNK
    ts = z_ref.shape[0]
    nc = ts // L
    npair = RW_HEADS // 2

    @pl.when(pl.program_id(1) == 0)
    def _():
        prev_sc[...] = jnp.zeros_like(prev_sc)
        p_sc[...] = jnp.zeros_like(p_sc)

    z = z_ref[...]
    row = _iota2(z.shape, 0)
    zprev = jnp.where(row == 0, prev_sc[0:1, :], pltpu.roll(z, 1, 0))
    prev_sc[...] = jnp.broadcast_to(z[ts - 1:ts, :], prev_sc.shape)
    z = z + (zprev - z) * mu_ref[...]

    r = z[:, 0:RW_DIM]
    k = z[:, RW_DIM:2 * RW_DIM]
    v = z[:, 2 * RW_DIM:3 * RW_DIM]
    lo = z[:, 3 * RW_DIM:3 * RW_DIM + RW_LORA_PAD]
    lane = _iota2(lo.shape, 1)
    lo = jnp.where(lane < 32, jnp.tanh(lo), jnp.where(lane < 64, lo, jax.nn.sigmoid(lo)))
    lora = _mm(lo, wl_ref[...])
    w = -_softplus(-(w0_ref[...] + lora[:, 0:RW_DIM])) - 0.5
    log_w = -jnp.exp(w)
    a = jax.nn.sigmoid(a0_ref[...] + lora[:, RW_DIM:2 * RW_DIM])
    g = lora[:, 2 * RW_DIM:3 * RW_DIM]

    hsum = hsum_ref[...]

    def head_sum(x):
        wd = hsum.shape[0]
        return jnp.concatenate([_mm(x[:, i:i + wd], hsum) for i in range(0, RW_DIM, wd)], axis=1)

    kk = k * kk_ref[...]
    kk = kk / jnp.maximum(jnp.sqrt(head_sum(kk * kk)), 1e-12)
    k = k * (1.0 + (a - 1.0) * ka_ref[...])
    bonus = head_sum(r * k * rk_ref[...]) * v
    b = kk * a

    tri = (_iota2((L, L), 0) >= _iota2((L, L), 1)).astype(BF16)
    pieces = jnp.concatenate(_split3(log_w), axis=1)
    cum = []
    for c in range(nc):
        s3 = jnp.dot(tri, pieces[c * L:(c + 1) * L, :], preferred_element_type=F32)
        cum.append(s3[:, 0:RW_DIM] + s3[:, RW_DIM:2 * RW_DIM] + s3[:, 2 * RW_DIM:])
    cum = jnp.concatenate(cum, axis=0)
    g_inv = jnp.exp(-cum)
    r_t = r * jnp.exp(cum)
    k_t = k * g_inv
    b_t = b * g_inv
    kp_t = kk * jnp.exp(cum - log_w)

    t_i = _iota2((L, 2 * HD), 0)
    lane = _iota2((L, 2 * HD), 1)
    s_i = lane & (HD - 1)
    left = lane < HD
    lower = t_i >= s_i
    strict = t_i > s_i
    eye = (t_i == s_i).astype(F32)
    levels = []
    bs = 1
    while bs < L:
        shift = bs.bit_length() - 1
        x, y = t_i >> shift, s_i >> shift
        levels.append(((x ^ y) == 1) & (x > y))
        bs *= 2

    chains = [(c, p) for c in range(nc) for p in range(npair)]
    cut = lambda arr, c, p: arr[c * L:(c + 1) * L, p * 2 * HD:(p + 1) * 2 * HD]

    rt, kpt, vv, qk, qb, ak, ab = [], [], [], [], [], [], []
    for c, p in chains:
        rt.append(cut(r_t, c, p))
        kpt.append(cut(kp_t, c, p))
        vv.append(cut(v, c, p))
        rhs = jnp.concatenate([_bd(cut(k_t, c, p), left), _bd(cut(b_t, c, p), left)], axis=0)
        sc = _mm_nt(jnp.concatenate([rt[-1], kpt[-1]], axis=0), rhs)
        qk.append(jnp.where(lower, sc[0:L, 0:2 * HD], 0.0))
        qb.append(jnp.where(lower, sc[0:L, 2 * HD:], 0.0))
        ak.append(jnp.where(strict, sc[L:, 0:2 * HD], 0.0))
        ab.append(jnp.where(strict, sc[L:, 2 * HD:], 0.0))

    tinv = [eye - jnp.where(levels[0], a, 0.0) for a in ab]
    for lvl in levels[1:]:
        xs = [_mm(jnp.where(lvl, a, 0.0), _bd(t, left)) for a, t in zip(ab, tinv)]
        tinv = [t - _mm(t, _bd(x, left)) for t, x in zip(tinv, xs)]

    qa = [_mm(jnp.concatenate([q, a], axis=0), _bd(v_, left)) for q, a, v_ in zip(qk, ak, vv)]
    ww = [_mm(t, jnp.concatenate([_bd(kp, left), _bd(q[L:], left)], axis=1))
          for t, kp, q in zip(tinv, kpt, qa)]
    qw = [_mm(q, jnp.concatenate([_bd(w[:, 0:2 * HD], left), _bd(w[:, 2 * HD:], left)], axis=1))
          for q, w in zip(qb, ww)]
    r_hat = [r_ - q[:, 0:2 * HD] for r_, q in zip(rt, qw)]
    o_loc = [q[0:L] - w[:, 2 * HD:] for q, w in zip(qa, qw)]

    m_st, n_st = [], []
    for i, (c, p) in enumerate(chains):
        cum_c = cut(cum, c, p)
        cum_end = cum_c[L - 1:L, :]
        g_end = jnp.exp(cum_end - cum_c)
        g1 = _mm_tn(cut(b, c, p) * g_end, ww[i])
        g2 = _mm_tn(cut(k, c, p) * g_end, vv[i])
        m_st.append(eye * jnp.exp(cum_end) - _unbd(g1[:, 0:2 * HD], left))
        n_st.append(_unbd(g2, left) - _unbd(g1[:, 2 * HD:], left))

    state = [p_sc[p] for p in range(npair)]
    for i, (c, p) in enumerate(chains):
        op = _mm(jnp.concatenate([r_hat[i], m_st[i]], axis=0), _bd(state[p], left))
        o_sc[c * L:(c + 1) * L, p * 2 * HD:(p + 1) * 2 * HD] = op[0:L] + o_loc[i]
        state[p] = op[L:] + n_st[i]
    for p in range(npair):
        p_sc[p] = state[p]
    o = o_sc[...]

    mean = head_sum(o) * (1.0 / HD)
    dev = o - mean
    var = head_sum(dev * dev) * (1.0 / HD)
    o = dev * lax.rsqrt(var + RW_GN_EPS) * lnw_ref[...] + lnb_ref[...]
    y_ref[...] = (o + bonus) * g


def _rwkv(zrw, mu, wl, w0, a0, k_k, k_a, r_k, ln_w, ln_b, hsum, ts=512):
    b, s, _ = zrw.shape
    ts = min(ts, s)
    const = lambda i, j: (0, 0)
    vec = pl.BlockSpec((1, RW_DIM), const)
    return pl.pallas_call(
        _rwkv_kernel,
        grid=(b, s // ts),
        in_specs=[
            pl.BlockSpec((None, ts, RW_IN_PAD), lambda i, j: (i, j, 0)),
            pl.BlockSpec((1, RW_IN_PAD), const),
            pl.BlockSpec((RW_LORA_PAD, 3 * RW_DIM), const),
            vec, vec, vec, vec, vec, vec, vec,
            pl.BlockSpec(hsum.shape, const),
        ],
        out_specs=pl.BlockSpec((None, ts, RW_DIM), lambda i, j: (i, j, 0)),
        out_shape=jax.ShapeDtypeStruct((b, s, RW_DIM), F32),
        scratch_shapes=[
            pltpu.VMEM((8, RW_IN_PAD), F32),
            pltpu.VMEM((RW_HEADS // 2, HD, 2 * HD), F32),
            pltpu.VMEM((ts, RW_DIM), F32),
        ],
        compiler_params=_params("arbitrary", "arbitrary"),
    )(zrw, mu, wl, w0, a0, k_k, k_a, r_k, ln_w, ln_b, hsum)


def _tail_kernel(yml_ref, ylru_ref, yrw_ref, x_ref, p_ref, wo_ref, gmix_ref, gpre_ref, w1_ref, w2_ref,
                 gffn_ref, wg_ref, wp_ref, gple_ref, o_ref, *, tf):
    mix = (_mm(yml_ref[...], wo_ref[0:ML_DIM, :])
           + _mm(ylru_ref[...], wo_ref[ML_DIM:ML_DIM + LRU_DIM, :])
           + _mm(yrw_ref[...], wo_ref[ML_DIM + LRU_DIM:, :]))
    x1 = x_ref[...] + _rms(mix, gmix_ref[...])
    h = _rms(x1, gpre_ref[...]).astype(BF16)
    acc = None
    for c in range(w1_ref.shape[1] // tf):
        u = jnp.maximum(jnp.dot(h, w1_ref[:, c * tf:(c + 1) * tf], preferred_element_type=F32), 0.0)
        part = _mm(u * u, w2_ref[c * tf:(c + 1) * tf, :])
        acc = part if acc is None else acc + part
    x2 = x1 + _rms(acc, gffn_ref[...])
    e = jax.nn.sigmoid(_mm(x2, wg_ref[...])) * _mm(p_ref[...], wp_ref[...])
    o_ref[...] = x2 + _rms(e, gple_ref[...])


def _tail(yml, ylru, yrw, x2, p_all, layer, wo, gmix, gpre, w1, w2, gffn, wg, wp, gple, tm=512, tf=1024):
    t, d = x2.shape
    dp = p_all.shape[1]
    const = lambda i: (0, 0)
    row = lambda i: (i, 0)
    p_row = lambda i: (layer * (t // tm) + i, 0)
    resident = lambda w: pl.BlockSpec(w.shape, const, pipeline_mode=pl.Buffered(1))
    vec = pl.BlockSpec((1, d), const)
    return pl.pallas_call(
        functools.partial(_tail_kernel, tf=tf),
        grid=(t // tm,),
        in_specs=[
            pl.BlockSpec((tm, ML_DIM), row),
            pl.BlockSpec((tm, LRU_DIM), row),
            pl.BlockSpec((tm, RW_DIM), row),
            pl.BlockSpec((tm, d), row),
            pl.BlockSpec((tm, dp), p_row),
            resident(wo), vec, vec, resident(w1), resident(w2), vec, resident(wg), resident(wp), vec,
        ],
        out_specs=pl.BlockSpec((tm, d), row),
        out_shape=jax.ShapeDtypeStruct((t, d), F32),
        compiler_params=_params("arbitrary"),
    )(yml, ylru, yrw, x2, p_all, wo, gmix, gpre, w1, w2, gffn, wg, wp, gple)


def _pad_cols(w, n):
    return jnp.pad(w, ((0, 0), (0, n - w.shape[1])))


def _block_diag(blocks):
    n, d, e = blocks.shape
    eye = jnp.eye(n, dtype=blocks.dtype)
    return (eye[:, None, :, None] * blocks[:, :, None, :]).reshape(n * d, n * e)


def _mixers(z_parts, b, s, ml_gate_bias, ml_head_norm, lru, rw):
    zml, zlru, zrw, gcol, grow = z_parts
    zml = zml.reshape(b, s, -1)
    zlru = zlru.reshape(b, s, -1)
    zrw = zrw.reshape(b, s, -1)
    gcol = gcol.reshape(b, s, -1)
    bias = ml_gate_bias.reshape(1, 2 * ML_HEADS)
    yml = _mlstm(zml, gcol, grow, _pad_cols(bias, GATE_PAD), bias.reshape(-1, 1),
                 ml_head_norm.reshape(1, -1))
    ylru = _rglru(zlru, *lru)
    yrw = _rwkv(zrw, *rw)
    return yml, ylru, yrw


def kernel(x, p, norm_mix_pre, norm_mix_post, norm_ffn_pre, norm_ffn_post, norm_ple, w_in, w_out, ml_gate_bias, ml_head_norm, lru_conv_w, lru_conv_b, lru_w_r, lru_b_r, lru_w_i, lru_b_i, lru_lambda, rw_mu, rw_w0, rw_w2, rw_a0, rw_a2, rw_g2, rw_k_k, rw_k_a, rw_r_k, rw_ln_w, rw_ln_b, ffn_w1, ffn_w2, ple_w_proj, ple_w_gate):
    b, s, d = x.shape
    depth = w_in.shape[0]
    t = b * s
    ml_in = 4 * ML_DIM + 2 * ML_HEADS
    lru_in = 2 * LRU_DIM
    head_id = jnp.arange(MXU_WIDTH) // HD
    hsum = (head_id[:, None] == head_id[None, :]).astype(BF16)
    row = lambda v: v.reshape(1, -1)

    x2 = x.reshape(t, d)
    for l in range(depth):
        wi = w_in[l]
        w_gate = wi[:, 4 * ML_DIM:ml_in]
        z_parts = _in_proj(
            x2, row(norm_mix_pre[l]),
            wi[:, :4 * ML_DIM].astype(BF16),
            wi[:, ml_in:ml_in + lru_in].astype(BF16),
            _pad_cols(wi[:, ml_in + lru_in:], RW_IN_PAD).astype(BF16),
            _pad_cols(w_gate, GATE_PAD).astype(BF16),
            w_gate.T.astype(BF16))

        lru = (lru_conv_w[l], row(lru_conv_b[l]),
               jnp.concatenate([_block_diag(lru_w_r[l]), _block_diag(lru_w_i[l])], axis=1).astype(BF16),
               jnp.concatenate([row(lru_b_r[l]), row(lru_b_i[l])], axis=1),
               row(lru_lambda[l]))

        w_lora = jnp.zeros((RW_LORA_PAD, 3 * RW_DIM), F32)
        w_lora = w_lora.at[0:32, 0:RW_DIM].set(rw_w2[l])
        w_lora = w_lora.at[32:64, RW_DIM:2 * RW_DIM].set(rw_a2[l])
        w_lora = w_lora.at[64:RW_LORA, 2 * RW_DIM:].set(rw_g2[l])
        rw = (_pad_cols(row(rw_mu[l]), RW_IN_PAD), w_lora.astype(BF16), row(rw_w0[l]), row(rw_a0[l]),
              row(rw_k_k[l]), row(rw_k_a[l]), row(rw_r_k[l]), row(rw_ln_w[l]), row(rw_ln_b[l]), hsum)

        yml, ylru, yrw = _mixers(z_parts, b, s, ml_gate_bias[l], ml_head_norm[l], lru, rw)
        x2 = _tail(yml.reshape(t, -1), ylru.reshape(t, -1), yrw.reshape(t, -1), x2,
                   p.reshape(depth * t, -1), l, w_out[l].astype(BF16), row(norm_mix_post[l]), row(norm_ffn_pre[l]),
                   ffn_w1[l].astype(BF16), ffn_w2[l].astype(BF16), row(norm_ffn_post[l]),
                   ple_w_gate[l].astype(BF16), ple_w_proj[l].astype(BF16), row(norm_ple[l]))
    return x2.reshape(b, s, d)
```

```python
import functools

import jax
import jax.numpy as jnp
from jax import lax
from jax.experimental import pallas as pl
from jax.experimental.pallas import tpu as pltpu

F32 = jnp.float32
BF16 = jnp.bfloat16

EPS = 1e-6
HD = 64
ML_HEADS = 4
ML_DIM = ML_HEADS * HD
LRU_DIM = 256
LRU_C = 8.0
CONV_W = 4
LRU_GROUP = 16
RW_HEADS = 8
RW_DIM = RW_HEADS * HD
RW_LORA = 160
RW_LORA_PAD = 256
RW_IN_PAD = 3 * RW_DIM + RW_LORA_PAD
RW_GN_EPS = 64e-5
CHUNK = 64
GATE_PAD = 128

VMEM_LIMIT = 48 * 1024 * 1024
MXU_WIDTH = 256

_NT = (((1,), (1,)), ((), ()))
_TN = (((0,), (0,)), ((), ()))


def _mm(a, b):
    return jnp.dot(a.astype(BF16), b.astype(BF16), preferred_element_type=F32)


def _mm_nt(a, b):
    return lax.dot_general(a.astype(BF16), b.astype(BF16), _NT, preferred_element_type=F32)


def _mm_tn(a, b):
    return lax.dot_general(a.astype(BF16), b.astype(BF16), _TN, preferred_element_type=F32)


def _mm_f32(a, b):
    return jnp.dot(a, b, preferred_element_type=F32, precision=lax.Precision.HIGHEST)


def _rms(x, gain):
    return x * lax.rsqrt(jnp.mean(x * x, axis=-1, keepdims=True) + EPS) * gain


def _softplus(x):
    return jnp.maximum(x, 0.0) + jnp.log(1.0 + jnp.exp(-jnp.abs(x)))


def _log_sigmoid(x):
    return jnp.minimum(x, 0.0) - jnp.log(1.0 + jnp.exp(-jnp.abs(x)))


def _split3(x):
    hi = x.astype(BF16)
    rest = x - hi.astype(F32)
    mid = rest.astype(BF16)
    return hi, mid, (rest - mid.astype(F32)).astype(BF16)


def _iota2(shape, dim):
    return lax.broadcasted_iota(jnp.int32, shape, dim)


def _params(*sem):
    return pltpu.CompilerParams(dimension_semantics=sem, vmem_limit_bytes=VMEM_LIMIT)


def _bd(y, left):
    return jnp.concatenate([jnp.where(left, y, 0.0), jnp.where(left, 0.0, y)], axis=0)


def _unbd(z, left):
    return jnp.where(left, z[0:HD, :], z[HD:2 * HD, :])


def _in_proj_kernel(x_ref, g_ref, wml_ref, wlru_ref, wrw_ref, wg_ref, wgt_ref,
                    zml_ref, zlru_ref, zrw_ref, gcol_ref, grow_ref):
    hb = _rms(x_ref[...], g_ref[...]).astype(BF16)
    zml_ref[...] = jnp.dot(hb, wml_ref[...], preferred_element_type=F32)
    zlru_ref[...] = jnp.dot(hb, wlru_ref[...], preferred_element_type=F32)
    zrw_ref[...] = jnp.dot(hb, wrw_ref[...], preferred_element_type=F32)
    gcol_ref[...] = jnp.dot(hb, wg_ref[...], preferred_element_type=F32)
    grow_ref[...] = lax.dot_general(wgt_ref[...], hb, _NT, preferred_element_type=F32)


def _in_proj(x2, gain, wml, wlru, wrw, wg, wgt, tm=1024):
    t, d = x2.shape
    const = lambda i: (0, 0)
    row = lambda i: (i, 0)
    resident = lambda w: pl.BlockSpec(w.shape, const, pipeline_mode=pl.Buffered(1))
    outs = (
        jax.ShapeDtypeStruct((t, wml.shape[1]), F32),
        jax.ShapeDtypeStruct((t, wlru.shape[1]), F32),
        jax.ShapeDtypeStruct((t, wrw.shape[1]), F32),
        jax.ShapeDtypeStruct((t, GATE_PAD), F32),
        jax.ShapeDtypeStruct((wgt.shape[0], t), F32),
    )
    return pl.pallas_call(
        _in_proj_kernel,
        grid=(t // tm,),
        in_specs=[
            pl.BlockSpec((tm, d), row),
            pl.BlockSpec((1, d), const),
            resident(wml), resident(wlru), resident(wrw), resident(wg), resident(wgt),
        ],
        out_specs=(
            pl.BlockSpec((tm, wml.shape[1]), row),
            pl.BlockSpec((tm, wlru.shape[1]), row),
            pl.BlockSpec((tm, wrw.shape[1]), row),
            pl.BlockSpec((tm, GATE_PAD), row),
            pl.BlockSpec((wgt.shape[0], tm), lambda i: (0, i)),
        ),
        out_shape=outs,
        compiler_params=_params("arbitrary"),
    )(x2, gain, wml, wlru, wrw, wg, wgt)


def _mlstm_kernel(z_ref, gcol_ref, grow_ref, bcol_ref, brow_ref, hn_ref, y_ref, st_sc, m_sc):
    L = CHUNK
    ts = z_ref.shape[0]
    nc = ts // L
    npair = ML_HEADS // 2
    pw = 2 * HD

    @pl.when(pl.program_id(1) == 0)
    def _():
        st_sc[...] = jnp.zeros_like(st_sc)
        m_sc[...] = jnp.zeros_like(m_sc)

    gc = gcol_ref[...] + bcol_ref[...]
    tri = (_iota2((L, L), 0) >= _iota2((L, L), 1)).astype(BF16)
    pieces = jnp.concatenate(_split3(_log_sigmoid(gc)), axis=1)
    f_col = []
    for c in range(nc):
        s3 = jnp.dot(tri, pieces[c * L:(c + 1) * L, :], preferred_element_type=F32)
        f_col.append(s3[:, 0:GATE_PAD] + s3[:, GATE_PAD:2 * GATE_PAD] + s3[:, 2 * GATE_PAD:])
    f_col = jnp.concatenate(f_col, axis=0)
    gr = grow_ref[...] + brow_ref[...]
    rt_i = _iota2((ts, ts), 0)
    ct_i = _iota2((ts, ts), 1)
    triu = ((rt_i <= ct_i) & ((rt_i // L) == (ct_i // L))).astype(BF16)
    s3 = jnp.dot(jnp.concatenate(_split3(_log_sigmoid(gr)), axis=0), triu, preferred_element_type=F32)
    ng = 2 * ML_HEADS
    f_row = s3[0:ng] + s3[ng:2 * ng] + s3[2 * ng:]
    g_rows = f_row[ML_HEADS:2 * ML_HEADS, :] - gr[0:ML_HEADS, :]
    g_up = pltpu.roll(g_rows, HD, 1)
    g_dn = pltpu.roll(g_rows, ts - HD, 1)

    t_i = _iota2((L, pw), 0)
    lane = _iota2((L, pw), 1)
    left = lane < HD
    left1 = left[0:1, :]
    causal = t_i >= (lane & (HD - 1))
    r2 = _iota2((pw, 2 * pw), 0)
    c2 = _iota2((pw, 2 * pw), 1)
    diag2 = (r2 // HD) == ((c2 // HD) % 2)
    top2 = r2 < HD
    ones_bd = diag2[:, 0:pw].astype(F32)
    ones_l = jnp.ones((L, pw), F32)
    scale = HD ** -0.5

    def pair_cols(arr, c, j0, j1):
        rows = arr[c * L:(c + 1) * L, :]
        return jnp.where(left, rows[:, j0:j0 + 1], rows[:, j1:j1 + 1])

    chains = [(p, c) for p in range(npair) for c in range(nc)]

    pre = []
    for p, c in chains:
        h0, h1 = 2 * p, 2 * p + 1
        rows = slice(c * L, (c + 1) * L)
        q = z_ref[rows, p * pw:(p + 1) * pw] * scale
        k = z_ref[rows, ML_DIM + p * pw:ML_DIM + (p + 1) * pw]
        v = z_ref[rows, 2 * ML_DIM + p * pw:2 * ML_DIM + (p + 1) * pw]
        fc = pair_cols(f_col, c, ML_HEADS + h0, ML_HEADS + h1)
        ic = pair_cols(gc, c, h0, h1)
        w0 = (c // 2) * pw
        if c % 2 == 0:
            g_pair = jnp.where(left1, g_rows[h0:h0 + 1, w0:w0 + pw], g_up[h1:h1 + 1, w0:w0 + pw])
        else:
            g_pair = jnp.where(left1, g_dn[h0:h0 + 1, w0:w0 + pw], g_rows[h1:h1 + 1, w0:w0 + pw])
        log_d = jnp.where(causal, fc - g_pair, -jnp.inf)
        d_max = jnp.where(left,
                          jnp.max(jnp.where(left, log_d, -jnp.inf), axis=-1, keepdims=True),
                          jnp.max(jnp.where(left, -jnp.inf, log_d), axis=-1, keepdims=True))
        s_raw = _mm_nt(q, _bd(k, left))
        f_tot = fc[L - 1:L, :]
        lw = f_tot - fc + ic
        m_loc = jnp.max(lw, axis=0, keepdims=True)
        kw = k * jnp.exp(lw - m_loc)
        new = _mm_tn(kw, jnp.concatenate([v, ones_l], axis=1))
        top = jnp.concatenate([_bd(v, left), ones_bd], axis=1)
        pre.append((q, fc, log_d, d_max, s_raw, f_tot, m_loc, new, top))

    lhs, rhs, m_rows = [], [], []
    for p in range(npair):
        state = st_sc[p]
        m_in = m_sc[p:p + 1, :]
        for c in range(nc):
            q, fc, log_d, d_max, s_raw, f_tot, m_loc, new, top = pre[p * nc + c]
            m_inter = fc + m_in
            m_row = jnp.maximum(m_inter, d_max)
            sd = s_raw * jnp.exp(log_d - m_row)
            lhs.append(jnp.concatenate([sd, jnp.exp(m_inter - m_row) * q], axis=1))
            rhs.append(jnp.concatenate([top, state], axis=0))
            m_rows.append(m_row)
            m_new = jnp.maximum(f_tot + m_in, m_loc)
            a = jnp.exp(f_tot + m_in - m_new)
            b = jnp.exp(m_loc - m_new)
            a_col = jnp.where(top2, a[:, 0:1], a[:, HD:HD + 1])
            b_col = jnp.where(top2, b[:, 0:1], b[:, HD:HD + 1])
            state = a_col * state + jnp.where(diag2, b_col * new, 0.0)
            m_in = m_new
        st_sc[p] = state
        m_sc[p:p + 1, :] = m_in

    for i, (p, c) in enumerate(chains):
        rows = slice(c * L, (c + 1) * L)
        nd = _mm(lhs[i], rhs[i])
        hh = nd[:, 0:pw] / jnp.maximum(jnp.abs(nd[:, pw:]), jnp.exp(-m_rows[i]))
        ms = _mm(hh * hh, ones_bd) * (1.0 / HD)
        o = z_ref[rows, 3 * ML_DIM + p * pw:3 * ML_DIM + (p + 1) * pw]
        y_ref[rows, p * pw:(p + 1) * pw] = ((hh * lax.rsqrt(ms + EPS) * hn_ref[:, p * pw:(p + 1) * pw])
                                            * jax.nn.sigmoid(o))


def _mlstm(zml, gcol, grow, bcol, brow, head_norm, ts=256):
    b, s, _ = zml.shape
    ts = min(ts, s)
    nblk = s // ts
    const = lambda i, j: (0, 0)
    return pl.pallas_call(
        _mlstm_kernel,
        grid=(b, nblk),
        in_specs=[
            pl.BlockSpec((None, ts, 4 * ML_DIM), lambda i, j: (i, j, 0)),
            pl.BlockSpec((None, ts, GATE_PAD), lambda i, j: (i, j, 0)),
            pl.BlockSpec((2 * ML_HEADS, ts), lambda i, j: (0, i * nblk + j)),
            pl.BlockSpec((1, GATE_PAD), const),
            pl.BlockSpec((2 * ML_HEADS, 1), const),
            pl.BlockSpec((1, ML_DIM), const),
        ],
        out_specs=pl.BlockSpec((None, ts, ML_DIM), lambda i, j: (i, j, 0)),
        out_shape=jax.ShapeDtypeStruct((b, s, ML_DIM), F32),
        scratch_shapes=[
            pltpu.VMEM((ML_HEADS // 2, 2 * HD, 4 * HD), F32),
            pltpu.VMEM((8, 2 * HD), F32),
        ],
        compiler_params=_params("arbitrary", "arbitrary"),
    )(zml, gcol, grow, bcol, brow, head_norm)


def _shift_rows(x, d, fill, row):
    return jnp.where(row >= d, pltpu.roll(x, d, 0), fill)


def _lru_kernel(z_ref, cw_ref, cb_ref, wg_ref, bg_ref, lam_ref, y_ref, xbuf, h_sc):
    ts = z_ref.shape[0]

    @pl.when(pl.program_id(1) == 0)
    def _():
        xbuf[0:8, :] = jnp.zeros((8, LRU_DIM), F32)
        h_sc[...] = jnp.zeros_like(h_sc)

    xb = z_ref[:, 0:LRU_DIM]
    gb = z_ref[:, LRU_DIM:2 * LRU_DIM]
    xbuf[8:8 + ts, :] = xb
    xc = cb_ref[...] + cw_ref[CONV_W - 1:CONV_W, :] * xb
    for j in range(1, CONV_W):
        xc = xc + cw_ref[CONV_W - 1 - j:CONV_W - j, :] * xbuf[8 - j:8 - j + ts, :]
    xbuf[0:8, :] = xb[ts - 8:ts, :]

    gates = _mm(xc, wg_ref[...]) + bg_ref[...]
    r = jax.nn.sigmoid(gates[:, 0:LRU_DIM])
    i = jax.nn.sigmoid(gates[:, LRU_DIM:2 * LRU_DIM])
    log_a = -LRU_C * r * _softplus(-lam_ref[...])
    a = jnp.exp(log_a)
    u = jnp.sqrt(-jnp.tanh(log_a) * (a * a + 1.0)) * (i * xc)

    row = _iota2((ts, LRU_DIM), 0) & (LRU_GROUP - 1)
    acc_a, acc_h = a, u
    d = 1
    while d < LRU_GROUP:
        acc_h = acc_a * _shift_rows(acc_h, d, 0.0, row) + acc_h
        acc_a = acc_a * _shift_rows(acc_a, d, 1.0, row)
        d *= 2
    gelu = 0.5 * gb * (1.0 + jnp.tanh(0.7978845608028654 * (gb + 0.044715 * (gb * gb * gb))))
    carry = h_sc[0:1, :]
    for g in range(ts // LRU_GROUP):
        rows = slice(g * LRU_GROUP, (g + 1) * LRU_GROUP)
        hg = acc_a[rows, :] * carry + acc_h[rows, :]
        y_ref[rows, :] = hg * gelu[rows, :]
        carry = hg[LRU_GROUP - 1:LRU_GROUP, :]
    h_sc[...] = jnp.broadcast_to(carry, h_sc.shape)


def _rglru(zlru, conv_w, conv_b, wg, bg, lam, ts=2048):
    b, s, _ = zlru.shape
    ts = min(ts, s)
    const = lambda i, j: (0, 0)
    return pl.pallas_call(
        _lru_kernel,
        grid=(b, s // ts),
        in_specs=[
            pl.BlockSpec((None, ts, 2 * LRU_DIM), lambda i, j: (i, j, 0)),
            pl.BlockSpec((CONV_W, LRU_DIM), const),
            pl.BlockSpec((1, LRU_DIM), const),
            pl.BlockSpec((LRU_DIM, 2 * LRU_DIM), const),
            pl.BlockSpec((1, 2 * LRU_DIM), const),
            pl.BlockSpec((1, LRU_DIM), const),
        ],
        out_specs=pl.BlockSpec((None, ts, LRU_DIM), lambda i, j: (i, j, 0)),
        out_shape=jax.ShapeDtypeStruct((b, s, LRU_DIM), F32),
        scratch_shapes=[
            pltpu.VMEM((ts + 8, LRU_DIM), F32),
            pltpu.VMEM((8, LRU_DIM), F32),
        ],
        compiler_params=_params("arbitrary", "arbitrary"),
    )(zlru, conv_w, conv_b, wg, bg, lam)


def _rwkv_kernel(z_ref, mu_ref, wl_ref, w0_ref, a0_ref, kk_ref, ka_ref, rk_ref, lnw_ref, lnb_ref,
                 hsum_ref, y_ref, prev_sc, p_sc, o_sc):
    L = CHUNK
    ts = z_ref.shape[0]
    nc = ts // L
    npair = RW_HEADS // 2

    @pl.when(pl.program_id(1) == 0)
    def _():
        prev_sc[...] = jnp.zeros_like(prev_sc)
        p_sc[...] = jnp.zeros_like(p_sc)

    z = z_ref[...]
    row = _iota2(z.shape, 0)
    zprev = jnp.where(row == 0, prev_sc[0:1, :], pltpu.roll(z, 1, 0))
    prev_sc[...] = jnp.broadcast_to(z[ts - 1:ts, :], prev_sc.shape)
    z = z + (zprev - z) * mu_ref[...]

    r = z[:, 0:RW_DIM]
    k = z[:, RW_DIM:2 * RW_DIM]
    v = z[:, 2 * RW_DIM:3 * RW_DIM]
    lo = z[:, 3 * RW_DIM:3 * RW_DIM + RW_LORA_PAD]
    lane = _iota2(lo.shape, 1)
    lo = jnp.where(lane < 32, jnp.tanh(lo), jnp.where(lane < 64, lo, jax.nn.sigmoid(lo)))
    lora = _mm(lo, wl_ref[...])
    w = -_softplus(-(w0_ref[...] + lora[:, 0:RW_DIM])) - 0.5
    log_w = -jnp.exp(w)
    a = jax.nn.sigmoid(a0_ref[...] + lora[:, RW_DIM:2 * RW_DIM])
    g = lora[:, 2 * RW_DIM:3 * RW_DIM]

    hsum = hsum_ref[...]

    def head_sum(x):
        wd = hsum.shape[0]
        return jnp.concatenate([_mm(x[:, i:i + wd], hsum) for i in range(0, RW_DIM, wd)], axis=1)

    kk = k * kk_ref[...]
    kk = kk / jnp.maximum(jnp.sqrt(head_sum(kk * kk)), 1e-12)
    k = k * (1.0 + (a - 1.0) * ka_ref[...])
    bonus = head_sum(r * k * rk_ref[...]) * v
    b = kk * a

    tri = (_iota2((L, L), 0) >= _iota2((L, L), 1)).astype(BF16)
    pieces = jnp.concatenate(_split3(log_w), axis=1)
    cum = []
    for c in range(nc):
        s3 = jnp.dot(tri, pieces[c * L:(c + 1) * L, :], preferred_element_type=F32)
        cum.append(s3[:, 0:RW_DIM] + s3[:, RW_DIM:2 * RW_DIM] + s3[:, 2 * RW_DIM:])
    cum = jnp.concatenate(cum, axis=0)
    g_inv = jnp.exp(-cum)
    r_t = r * jnp.exp(cum)
    k_t = k * g_inv
    b_t = b * g_inv
    kp_t = kk * jnp.exp(cum - log_w)

    t_i = _iota2((L, 2 * HD), 0)
    lane = _iota2((L, 2 * HD), 1)
    s_i = lane & (HD - 1)
    left = lane < HD
    lower = t_i >= s_i
    strict = t_i > s_i
    eye = (t_i == s_i).astype(F32)
    levels = []
    bs = 1
    while bs < L:
        shift = bs.bit_length() - 1
        x, y = t_i >> shift, s_i >> shift
        levels.append(((x ^ y) == 1) & (x > y))
        bs *= 2

    chains = [(c, p) for c in range(nc) for p in range(npair)]
    cut = lambda arr, c, p: arr[c * L:(c + 1) * L, p * 2 * HD:(p + 1) * 2 * HD]

    rt, kpt, vv, qk, qb, ak, ab = [], [], [], [], [], [], []
    for c, p in chains:
        rt.append(cut(r_t, c, p))
        kpt.append(cut(kp_t, c, p))
        vv.append(cut(v, c, p))
        rhs = jnp.concatenate([_bd(cut(k_t, c, p), left), _bd(cut(b_t, c, p), left)], axis=0)
        sc = _mm_nt(jnp.concatenate([rt[-1], kpt[-1]], axis=0), rhs)
        qk.append(jnp.where(lower, sc[0:L, 0:2 * HD], 0.0))
        qb.append(jnp.where(lower, sc[0:L, 2 * HD:], 0.0))
        ak.append(jnp.where(strict, sc[L:, 0:2 * HD], 0.0))
        ab.append(jnp.where(strict, sc[L:, 2 * HD:], 0.0))

    tinv = [eye - jnp.where(levels[0], a, 0.0) for a in ab]
    for lvl in levels[1:]:
        xs = [_mm(jnp.where(lvl, a, 0.0), _bd(t, left)) for a, t in zip(ab, tinv)]
        tinv = [t - _mm(t, _bd(x, left)) for t, x in zip(tinv, xs)]

    qa = [_mm(jnp.concatenate([q, a], axis=0), _bd(v_, left)) for q, a, v_ in zip(qk, ak, vv)]
    ww = [_mm(t, jnp.concatenate([_bd(kp, left), _bd(q[L:], left)], axis=1))
          for t, kp, q in zip(tinv, kpt, qa)]
    qw = [_mm(q, jnp.concatenate([_bd(w[:, 0:2 * HD], left), _bd(w[:, 2 * HD:], left)], axis=1))
          for q, w in zip(qb, ww)]
    r_hat = [r_ - q[:, 0:2 * HD] for r_, q in zip(rt, qw)]
    o_loc = [q[0:L] - w[:, 2 * HD:] for q, w in zip(qa, qw)]

    m_st, n_st = [], []
    for i, (c, p) in enumerate(chains):
        cum_c = cut(cum, c, p)
        cum_end = cum_c[L - 1:L, :]
        g_end = jnp.exp(cum_end - cum_c)
        g1 = _mm_tn(cut(b, c, p) * g_end, ww[i])
        g2 = _mm_tn(cut(k, c, p) * g_end, vv[i])
        m_st.append(eye * jnp.exp(cum_end) - _unbd(g1[:, 0:2 * HD], left))
        n_st.append(_unbd(g2, left) - _unbd(g1[:, 2 * HD:], left))

    state = [p_sc[p] for p in range(npair)]
    for i, (c, p) in enumerate(chains):
        op = _mm(jnp.concatenate([r_hat[i], m_st[i]], axis=0), _bd(state[p], left))
        o_sc[c * L:(c + 1) * L, p * 2 * HD:(p + 1) * 2 * HD] = op[0:L] + o_loc[i]
        state[p] = op[L:] + n_st[i]
    for p in range(npair):
        p_sc[p] = state[p]
    o = o_sc[...]

    mean = head_sum(o) * (1.0 / HD)
    dev = o - mean
    var = head_sum(dev * dev) * (1.0 / HD)
    o = dev * lax.rsqrt(var + RW_GN_EPS) * lnw_ref[...] + lnb_ref[...]
    y_ref[...] = (o + bonus) * g


def _rwkv(zrw, mu, wl, w0, a0, k_k, k_a, r_k, ln_w, ln_b, hsum, ts=512):
    b, s, _ = zrw.shape
    ts = min(ts, s)
    const = lambda i, j: (0, 0)
    vec = pl.BlockSpec((1, RW_DIM), const)
    return pl.pallas_call(
        _rwkv_kernel,
        grid=(b, s // ts),
        in_specs=[
            pl.BlockSpec((None, ts, RW_IN_PAD), lambda i, j: (i, j, 0)),
            pl.BlockSpec((1, RW_IN_PAD), const),
            pl.BlockSpec((RW_LORA_PAD, 3 * RW_DIM), const),
            vec, vec, vec, vec, vec, vec, vec,
            pl.BlockSpec(hsum.shape, const),
        ],
        out_specs=pl.BlockSpec((None, ts, RW_DIM), lambda i, j: (i, j, 0)),
        out_shape=jax.ShapeDtypeStruct((b, s, RW_DIM), F32),
        scratch_shapes=[
            pltpu.VMEM((8, RW_IN_PAD), F32),
            pltpu.VMEM((RW_HEADS // 2, HD, 2 * HD), F32),
            pltpu.VMEM((ts, RW_DIM), F32),
        ],
        compiler_params=_params("arbitrary", "arbitrary"),
    )(zrw, mu, wl, w0, a0, k_k, k_a, r_k, ln_w, ln_b, hsum)


def _tail_kernel(yml_ref, ylru_ref, yrw_ref, x_ref, p_ref, wo_ref, gmix_ref, gpre_ref, w1_ref, w2_ref,
                 gffn_ref, wg_ref, wp_ref, gple_ref, o_ref, *, tf):
    mix = (_mm(yml_ref[...], wo_ref[0:ML_DIM, :])
           + _mm(ylru_ref[...], wo_ref[ML_DIM:ML_DIM + LRU_DIM, :])
           + _mm(yrw_ref[...], wo_ref[ML_DIM + LRU_DIM:, :]))
    x1 = x_ref[...] + _rms(mix, gmix_ref[...])
    h = _rms(x1, gpre_ref[...]).astype(BF16)
    acc = None
    for c in range(w1_ref.shape[1] // tf):
        u = jnp.maximum(jnp.dot(h, w1_ref[:, c * tf:(c + 1) * tf], preferred_element_type=F32), 0.0)
        part = _mm(u * u, w2_ref[c * tf:(c + 1) * tf, :])
        acc = part if acc is None else acc + part
    x2 = x1 + _rms(acc, gffn_ref[...])
    e = jax.nn.sigmoid(_mm(x2, wg_ref[...])) * _mm(p_ref[...], wp_ref[...])
    o_ref[...] = x2 + _rms(e, gple_ref[...])


def _tail(yml, ylru, yrw, x2, p_all, layer, wo, gmix, gpre, w1, w2, gffn, wg, wp, gple, tm=512, tf=1024):
    t, d = x2.shape
    dp = p_all.shape[1]
    const = lambda i: (0, 0)
    row = lambda i: (i, 0)
    p_row = lambda i: (layer * (t // tm) + i, 0)
    resident = lambda w: pl.BlockSpec(w.shape, const, pipeline_mode=pl.Buffered(1))
    vec = pl.BlockSpec((1, d), const)
    return pl.pallas_call(
        functools.partial(_tail_kernel, tf=tf),
        grid=(t // tm,),
        in_specs=[
            pl.BlockSpec((tm, ML_DIM), row),
            pl.BlockSpec((tm, LRU_DIM), row),
            pl.BlockSpec((tm, RW_DIM), row),
            pl.BlockSpec((tm, d), row),
            pl.BlockSpec((tm, dp), p_row),
            resident(wo), vec, vec, resident(w1), resident(w2), vec, resident(wg), resident(wp), vec,
        ],
        out_specs=pl.BlockSpec((tm, d), row),
        out_shape=jax.ShapeDtypeStruct((t, d), F32),
        compiler_params=_params("arbitrary"),
    )(yml, ylru, yrw, x2, p_all, wo, gmix, gpre, w1, w2, gffn, wg, wp, gple)


def _pad_cols(w, n):
    return jnp.pad(w, ((0, 0), (0, n - w.shape[1])))


def _block_diag(blocks):
    n, d, e = blocks.shape
    eye = jnp.eye(n, dtype=blocks.dtype)
    return (eye[:, None, :, None] * blocks[:, :, None, :]).reshape(n * d, n * e)


def _mixers(z_parts, b, s, ml_gate_bias, ml_head_norm, lru, rw):
    zml, zlru, zrw, gcol, grow = z_parts
    zml = zml.reshape(b, s, -1)
    zlru = zlru.reshape(b, s, -1)
    zrw = zrw.reshape(b, s, -1)
    gcol = gcol.reshape(b, s, -1)
    bias = ml_gate_bias.reshape(1, 2 * ML_HEADS)
    yml = _mlstm(zml, gcol, grow, _pad_cols(bias, GATE_PAD), bias.reshape(-1, 1),
                 ml_head_norm.reshape(1, -1))
    ylru = _rglru(zlru, *lru)
    yrw = _rwkv(zrw, *rw)
    return yml, ylru, yrw


def kernel(x, p, norm_mix_pre, norm_mix_post, norm_ffn_pre, norm_ffn_post, norm_ple, w_in, w_out, ml_gate_bias, ml_head_norm, lru_conv_w, lru_conv_b, lru_w_r, lru_b_r, lru_w_i, lru_b_i, lru_lambda, rw_mu, rw_w0, rw_w2, rw_a0, rw_a2, rw_g2, rw_k_k, rw_k_a, rw_r_k, rw_ln_w, rw_ln_b, ffn_w1, ffn_w2, ple_w_proj, ple_w_gate):
    b, s, d = x.shape
    depth = w_in.shape[0]
    t = b * s
    ml_in = 4 * ML_DIM + 2 * ML_HEADS
    lru_in = 2 * LRU_DIM
    head_id = jnp.arange(MXU_WIDTH) // HD
    hsum = (head_id[:, None] == head_id[None, :]).astype(BF16)
    row = lambda v: v.reshape(1, -1)

    x2 = x.reshape(t, d)
    for l in range(depth):
        wi = w_in[l]
        w_gate = wi[:, 4 * ML_DIM:ml_in]
        z_parts = _in_proj(
            x2, row(norm_mix_pre[l]),
            wi[:, :4 * ML_DIM].astype(BF16),
            wi[:, ml_in:ml_in + lru_in].astype(BF16),
            _pad_cols(wi[:, ml_in + lru_in:], RW_IN_PAD).astype(BF16),
            _pad_cols(w_gate, GATE_PAD).astype(BF16),
            w_gate.T.astype(BF16))

        lru = (lru_conv_w[l], row(lru_conv_b[l]),
               jnp.concatenate([_block_diag(lru_w_r[l]), _block_diag(lru_w_i[l])], axis=1).astype(BF16),
               jnp.concatenate([row(lru_b_r[l]), row(lru_b_i[l])], axis=1),
               row(lru_lambda[l]))

        w_lora = jnp.zeros((RW_LORA_PAD, 3 * RW_DIM), F32)
        w_lora = w_lora.at[0:32, 0:RW_DIM].set(rw_w2[l])
        w_lora = w_lora.at[32:64, RW_DIM:2 * RW_DIM].set(rw_a2[l])
        w_lora = w_lora.at[64:RW_LORA, 2 * RW_DIM:].set(rw_g2[l])
        rw = (_pad_cols(row(rw_mu[l]), RW_IN_PAD), w_lora.astype(BF16), row(rw_w0[l]), row(rw_a0[l]),
              row(rw_k_k[l]), row(rw_k_a[l]), row(rw_r_k[l]), row(rw_ln_w[l]), row(rw_ln_b[l]), hsum)

        yml, ylru, yrw = _mixers(z_parts, b, s, ml_gate_bias[l], ml_head_norm[l], lru, rw)
        x2 = _tail(yml.reshape(t, -1), ylru.reshape(t, -1), yrw.reshape(t, -1), x2,
                   p.reshape(depth * t, -1), l, w_out[l].astype(BF16), row(norm_mix_post[l]), row(norm_ffn_pre[l]),
                   ffn_w1[l].astype(BF16), ffn_w2[l].astype(BF16), row(norm_ffn_post[l]),
                   ple_w_gate[l].astype(BF16), ple_w_proj[l].astype(BF16), row(norm_ple[l]))
    return x2.reshape(b, s, d)
```
